```python
import jax, jax.numpy as jnp
from jax import lax
import numpy as np

D_MODEL = 2048
BATCH = 2
SEQ = 4096
DEPTH = 1
DEC_BATCH = 128
DEC_SEQ = 4
PAST_LEN = 8192
PAGE_SIZE = 128

ROPE_THETA = 500000.0
EPS = 1e-6
Q_BLOCK = 128
MLA_HEADS = 8
MLA_D_NOPE = 128
MLA_D_ROPE = 64
MLA_D_QK = MLA_D_NOPE + MLA_D_ROPE
MLA_D_V = 128
MLA_D_CQ = 768
MLA_D_CKV = 512
DSA_HEADS = 8
DSA_KV_HEADS = 2
DSA_GROUP = DSA_HEADS // DSA_KV_HEADS
DSA_HEAD_DIM = 128
DSA_ROT = DSA_HEAD_DIM // 4
IDX_HEADS = 16
IDX_DIM = 64
IDX_ROT = IDX_DIM // 4
TOPK_MAX = 256
MEM_TOKENS = 256
MEM_HEADS = 4
MEM_HEAD_DIM = 256
D_FF = 5632
N_BRANCH = 3
IN_SPLITS = (
    ("mla_cq", MLA_D_CQ),
    ("mla_ckv", MLA_D_CKV),
    ("mla_kr", MLA_D_ROPE),
    ("dsa_q", DSA_HEADS * DSA_HEAD_DIM),
    ("dsa_k", DSA_KV_HEADS * DSA_HEAD_DIM),
    ("dsa_v", DSA_KV_HEADS * DSA_HEAD_DIM),
    ("idx_q", IDX_HEADS * IDX_DIM),
    ("idx_w", IDX_HEADS),
    ("idx_k", IDX_DIM),
    ("mem_q", MEM_HEADS * MEM_HEAD_DIM),
    ("gates", N_BRANCH * D_MODEL),
)
D_IN = sum(w for _, w in IN_SPLITS)

kernel_name = "hybrid_mla_dsa_memory_macaron_step"


def rms_norm(x, g):
    xf = x.astype(jnp.float32)
    y = xf * lax.rsqrt(jnp.mean(xf * xf, axis=-1, keepdims=True) + EPS)
    return (y * g.astype(jnp.float32)).astype(x.dtype)


def rope(x, pos, rot_dim):
    half = rot_dim // 2
    inv_freq = ROPE_THETA ** (-jnp.arange(half, dtype=jnp.float32) / half)
    ang = pos.astype(jnp.float32)[:, None] * inv_freq[None, :]
    cos = jnp.cos(ang)[None, :, None, :]
    sin = jnp.sin(ang)[None, :, None, :]
    xr = x[..., :rot_dim].astype(jnp.float32)
    x1, x2 = xr[..., :half], xr[..., half:]
    rot = jnp.concatenate([x1 * cos - x2 * sin, x1 * sin + x2 * cos], axis=-1)
    return jnp.concatenate([rot.astype(x.dtype), x[..., rot_dim:]], axis=-1)


def swiglu(x, w_gate, w_up, w_down):
    return (jax.nn.silu(x @ w_gate) * (x @ w_up)) @ w_down


def ffn_half(x, g, w_gate, w_up, w_down):
    return x + 0.5 * swiglu(rms_norm(x, g), w_gate, w_up, w_down)


def split_cols(h):
    out, off = {}, 0
    for name, w in IN_SPLITS:
        out[name] = h[..., off:off + w]
        off += w
    return out


def project_mixers(h, pos, p):
    B, T, _ = h.shape
    c = split_cols(h @ p["w_in"])
    cq = rms_norm(c["mla_cq"], p["g_mla_cq"])
    q = rms_norm((cq @ p["w_mla_uq"]).reshape(B, T, MLA_HEADS, MLA_D_QK), p["g_mla_q"])
    q_rope = rope(q[..., MLA_D_NOPE:], pos, MLA_D_ROPE)
    q_lat = jnp.einsum("bthn,hnc->bthc", q[..., :MLA_D_NOPE], p["w_mla_uk"])
    ckv = rms_norm(c["mla_ckv"], p["g_mla_ckv"])
    krope = rope(rms_norm(c["mla_kr"], p["g_mla_kr"])[:, :, None, :], pos, MLA_D_ROPE)[:, :, 0]
    dsa_q = rope(rms_norm(c["dsa_q"].reshape(B, T, DSA_HEADS, DSA_HEAD_DIM), p["g_dsa_q"]), pos, DSA_ROT)
    dsa_k = rope(rms_norm(c["dsa_k"].reshape(B, T, DSA_KV_HEADS, DSA_HEAD_DIM), p["g_dsa_k"]), pos, DSA_ROT)
    dsa_v = c["dsa_v"].reshape(B, T, DSA_KV_HEADS, DSA_HEAD_DIM)
    idx_q = rope(c["idx_q"].reshape(B, T, IDX_HEADS, IDX_DIM), pos, IDX_ROT)
    idx_k = rope(c["idx_k"][:, :, None, :], pos, IDX_ROT)[:, :, 0]
    mem_q = rms_norm(c["mem_q"].reshape(B, T, MEM_HEADS, MEM_HEAD_DIM), p["g_mem_q"])
    gates = jax.nn.sigmoid(c["gates"].astype(jnp.float32)).astype(h.dtype).reshape(B, T, N_BRANCH, D_MODEL)
    return {"q_lat": q_lat, "q_rope": q_rope, "ckv": ckv, "krope": krope,
            "dsa_q": dsa_q, "dsa_k": dsa_k, "dsa_v": dsa_v,
            "idx_q": idx_q, "idx_w": c["idx_w"], "idx_k": idx_k,
            "mem_q": mem_q, "gates": gates}


def memory_kv(mem, p):
    B, M, _ = mem.shape
    kv = (rms_norm(mem, p["g_mem_in"]) @ p["w_mem_kv"]).reshape(B, M, 2, MEM_HEADS, MEM_HEAD_DIM)
    return rms_norm(kv[:, :, 0], p["g_mem_k"]), kv[:, :, 1]


def memory_attend(mq, mk, mv):
    s = jnp.einsum("bthd,bmhd->bhtm", mq, mk).astype(jnp.float32) * (MEM_HEAD_DIM ** -0.5)
    pr = jax.nn.softmax(s, axis=-1).astype(mv.dtype)
    return jnp.einsum("bhtm,bmhd->bthd", pr, mv)


def mla_scores(q_lat, q_rope, ckv, krope):
    s = jnp.einsum("bqhc,bkc->bhqk", q_lat, ckv) + jnp.einsum("bqhr,bkr->bhqk", q_rope, krope)
    return s.astype(jnp.float32) * (MLA_D_QK ** -0.5)


def index_scores(iq, iw, ik):
    logits = jax.nn.relu(jnp.einsum("bqhd,bkd->bqhk", iq, ik).astype(jnp.float32))
    return jnp.einsum("bqh,bqhk->bqk", iw.astype(jnp.float32), logits)


def sparse_attend(q, ks, vs, valid):
    B, Q, H, D = q.shape
    qg = q.reshape(B, Q, DSA_KV_HEADS, DSA_GROUP, D)
    s = jnp.einsum("bqgrd,bqkgd->bqgrk", qg, ks).astype(jnp.float32) * (D ** -0.5)
    s = jnp.where(valid[:, :, None, None, :], s, -jnp.inf)
    pr = jax.nn.softmax(s, axis=-1).astype(vs.dtype)
    return jnp.einsum("bqgrk,bqkgd->bqgrd", pr, vs).reshape(B, Q, H, D)


def merge_branches(o_mla, o_dsa, o_mem, gates, p):
    B, T = o_mla.shape[:2]
    y_a = o_mla.reshape(B, T, -1) @ p["w_o_mla"]
    y_b = o_dsa.reshape(B, T, -1) @ p["w_o_dsa"]
    y_c = o_mem.reshape(B, T, -1) @ p["w_o_mem"]
    merged = gates[:, :, 0] * y_a + gates[:, :, 1] * y_b + gates[:, :, 2] * y_c
    return merged @ p["w_out"]


def prompt_core(pr, mem_k, mem_v, w_uv):
    ckv, krope, dk, dv, ik = pr["ckv"], pr["krope"], pr["dsa_k"], pr["dsa_v"], pr["idx_k"]
    B, T = ckv.shape[:2]
    nb = T // Q_BLOCK
    k_sel = min(TOPK_MAX, T // 4)
    key_pos = jnp.arange(T)
    take = jax.vmap(lambda a, i: a[i])

    def to_blocks(a):
        return jnp.moveaxis(a.reshape((B, nb, Q_BLOCK) + a.shape[2:]), 1, 0)

    def from_blocks(a):
        return jnp.moveaxis(a, 0, 1).reshape((B, T) + a.shape[3:])

    def block(args):
        start, q_lat, q_rope, dq, iq, iw = args
        q_pos = start + jnp.arange(Q_BLOCK)
        causal = key_pos[None, :] <= q_pos[:, None]
        s = jnp.where(causal[None, None], mla_scores(q_lat, q_rope, ckv, krope), -jnp.inf)
        pa = jax.nn.softmax(s, axis=-1).astype(ckv.dtype)
        o_lat = jnp.einsum("bhqk,bkc->bqhc", pa, ckv)
        isc = jnp.where(causal[None], index_scores(iq, iw, ik), -jnp.inf)
        _, sel = lax.top_k(isc, k_sel)
        valid = sel <= q_pos[None, :, None]
        return o_lat, sparse_attend(dq, take(dk, sel), take(dv, sel), valid)

    starts = jnp.arange(nb, dtype=jnp.int32) * Q_BLOCK
    o_lat, o_dsa = lax.map(block, (starts, to_blocks(pr["q_lat"]), to_blocks(pr["q_rope"]),
                                   to_blocks(pr["dsa_q"]), to_blocks(pr["idx_q"]), to_blocks(pr["idx_w"])))
    o_mla = jnp.einsum("bthc,hec->bthe", from_blocks(o_lat), w_uv)
    o_mem = memory_attend(pr["mem_q"], mem_k, mem_v)
    return o_mla, from_blocks(o_dsa), o_mem


def sample_core(pr, mem_k, mem_v, w_uv, layer, cache_ckv, cache_kr, cache_dk, cache_dv, cache_ik, page_table):
    ckv_new, kr_new, dk_new, dv_new, ik_new = pr["ckv"], pr["krope"], pr["dsa_k"], pr["dsa_v"], pr["idx_k"]
    DB, T = ckv_new.shape[:2]
    k_sel = min(TOPK_MAX, (PAST_LEN + T) // 4)
    q_pos = PAST_LEN + jnp.arange(T)
    new_causal = jnp.arange(T)[None, :] <= jnp.arange(T)[:, None]
    take = jax.vmap(lambda a, i: a[i])

    def gather_past(c):
        return c[layer, page_table].reshape((DB, PAST_LEN) + c.shape[3:])

    ckv_past = gather_past(cache_ckv)
    kr_past = gather_past(cache_kr)
    ik_past = gather_past(cache_ik)
    s = jnp.concatenate([
        mla_scores(pr["q_lat"], pr["q_rope"], ckv_past, kr_past),
        jnp.where(new_causal[None, None], mla_scores(pr["q_lat"], pr["q_rope"], ckv_new, kr_new), -jnp.inf),
    ], axis=-1)
    pa = jax.nn.softmax(s, axis=-1).astype(ckv_new.dtype)
    o_lat = (jnp.einsum("bhqk,bkc->bqhc", pa[..., :PAST_LEN], ckv_past)
             + jnp.einsum("bhqk,bkc->bqhc", pa[..., PAST_LEN:], ckv_new))
    o_mla = jnp.einsum("bthc,hec->bthe", o_lat, w_uv)
    isc = jnp.concatenate([
        index_scores(pr["idx_q"], pr["idx_w"], ik_past),
        jnp.where(new_causal[None], index_scores(pr["idx_q"], pr["idx_w"], ik_new), -jnp.inf),
    ], axis=-1)
    _, sel = lax.top_k(isc, k_sel)
    valid = sel <= q_pos[None, :, None]
    in_past = (sel < PAST_LEN)[..., None, None]
    past_idx = jnp.minimum(sel, PAST_LEN - 1)
    phys = take(page_table, past_idx // PAGE_SIZE)
    slot = past_idx % PAGE_SIZE
    new_idx = jnp.clip(sel - PAST_LEN, 0, T - 1)
    ks = jnp.where(in_past, cache_dk[layer, phys, slot], take(dk_new, new_idx))
    vs = jnp.where(in_past, cache_dv[layer, phys, slot], take(dv_new, new_idx))
    o_dsa = sparse_attend(pr["dsa_q"], ks, vs, valid)
    o_mem = memory_attend(pr["mem_q"], mem_k, mem_v)
    return o_mla, o_dsa, o_mem


def setup_inputs(seed: int = 0) -> dict:
    key = jax.random.key(seed)
    ks = iter(jax.random.split(key, 64))
    f32 = jnp.float32

    def nrm(shape, scale=1.0):
        return jax.random.normal(next(ks), shape, dtype=f32) * scale

    def w(shape, fan_in):
        return nrm((DEPTH,) + shape, fan_in ** -0.5)

    def gain(dim):
        return 1.0 + nrm((DEPTH, dim), 0.01)

    n_pages = PAST_LEN // PAGE_SIZE
    n_used = DEC_BATCH * n_pages
    n_pool = n_used + n_used // 4
    perm = jax.random.permutation(next(ks), n_pool)
    page_table = perm[:n_used].reshape(DEC_BATCH, n_pages).astype(jnp.int32)
    return {
        "x_prompt": nrm((BATCH, SEQ, D_MODEL)),
        "x_sample": nrm((DEC_BATCH, DEC_SEQ, D_MODEL)),
        "cache_mla_ckv": nrm((DEPTH, n_pool, PAGE_SIZE, MLA_D_CKV)),
        "cache_mla_krope": nrm((DEPTH, n_pool, PAGE_SIZE, MLA_D_ROPE)),
        "cache_dsa_k": nrm((DEPTH, n_pool, PAGE_SIZE, DSA_KV_HEADS, DSA_HEAD_DIM)),
        "cache_dsa_v": nrm((DEPTH, n_pool, PAGE_SIZE, DSA_KV_HEADS, DSA_HEAD_DIM)),
        "cache_idx_k": nrm((DEPTH, n_pool, PAGE_SIZE, IDX_DIM)),
        "cache_mem_k": nrm((DEPTH, DEC_BATCH, MEM_TOKENS, MEM_HEADS, MEM_HEAD_DIM)),
        "cache_mem_v": nrm((DEPTH, DEC_BATCH, MEM_TOKENS, MEM_HEADS, MEM_HEAD_DIM)),
        "page_table": page_table,
        "mem_prompt": nrm((BATCH, MEM_TOKENS, D_MODEL)),
        "g_ffn1": gain(D_MODEL),
        "w_ffn1_gate": w((D_MODEL, D_FF), D_MODEL),
        "w_ffn1_up": w((D_MODEL, D_FF), D_MODEL),
        "w_ffn1_down": w((D_FF, D_MODEL), D_FF),
        "g_mix": gain(D_MODEL),
        "w_in": w((D_MODEL, D_IN), D_MODEL),
        "g_mla_cq": gain(MLA_D_CQ),
        "w_mla_uq": w((MLA_D_CQ, MLA_HEADS * MLA_D_QK), MLA_D_CQ),
        "g_mla_q": gain(MLA_D_QK),
        "w_mla_uk": w((MLA_HEADS, MLA_D_NOPE, MLA_D_CKV), MLA_D_CKV),
        "w_mla_uv": w((MLA_HEADS, MLA_D_V, MLA_D_CKV), MLA_D_CKV),
        "g_mla_ckv": gain(MLA_D_CKV),
        "g_mla_kr": gain(MLA_D_ROPE),
        "g_dsa_q": gain(DSA_HEAD_DIM),
        "g_dsa_k": gain(DSA_HEAD_DIM),
        "g_mem_in": gain(D_MODEL),
        "w_mem_kv": w((D_MODEL, 2 * MEM_HEADS * MEM_HEAD_DIM), D_MODEL),
        "g_mem_q": gain(MEM_HEAD_DIM),
        "g_mem_k": gain(MEM_HEAD_DIM),
        "w_o_mla": w((MLA_HEADS * MLA_D_V, D_MODEL), MLA_HEADS * MLA_D_V),
        "w_o_dsa": w((DSA_HEADS * DSA_HEAD_DIM, D_MODEL), DSA_HEADS * DSA_HEAD_DIM),
        "w_o_mem": w((MEM_HEADS * MEM_HEAD_DIM, D_MODEL), MEM_HEADS * MEM_HEAD_DIM),
        "w_out": w((D_MODEL, D_MODEL), D_MODEL),
        "g_ffn2": gain(D_MODEL),
        "w_ffn2_gate": w((D_MODEL, D_FF), D_MODEL),
        "w_ffn2_up": w((D_MODEL, D_FF), D_MODEL),
        "w_ffn2_down": w((D_FF, D_MODEL), D_FF),
    }


def reference(x_prompt, x_sample, cache_mla_ckv, cache_mla_krope, cache_dsa_k, cache_dsa_v, cache_idx_k,
              cache_mem_k, cache_mem_v, page_table, mem_prompt,
              g_ffn1, w_ffn1_gate, w_ffn1_up, w_ffn1_down, g_mix, w_in,
              g_mla_cq, w_mla_uq, g_mla_q, w_mla_uk, w_mla_uv, g_mla_ckv, g_mla_kr,
              g_dsa_q, g_dsa_k, g_mem_in, w_mem_kv, g_mem_q, g_mem_k,
              w_o_mla, w_o_dsa, w_o_mem, w_out, g_ffn2, w_ffn2_gate, w_ffn2_up, w_ffn2_down):
    pos_p = jnp.arange(x_prompt.shape[1])
    pos_s = PAST_LEN + jnp.arange(x_sample.shape[1])
    xp, xs = x_prompt, x_sample
    st = {name: [] for name in ("ckv_p", "kr_p", "dk_p", "dv_p", "ik_p", "mk_p", "mv_p",
                                "ckv_s", "kr_s", "dk_s", "dv_s", "ik_s")}
    for l in range(DEPTH):
        p = {"w_in": w_in[l], "g_mla_cq": g_mla_cq[l], "w_mla_uq": w_mla_uq[l], "g_mla_q": g_mla_q[l],
             "w_mla_uk": w_mla_uk[l], "g_mla_ckv": g_mla_ckv[l], "g_mla_kr": g_mla_kr[l],
             "g_dsa_q": g_dsa_q[l], "g_dsa_k": g_dsa_k[l], "g_mem_in": g_mem_in[l], "w_mem_kv": w_mem_kv[l],
             "g_mem_q": g_mem_q[l], "g_mem_k": g_mem_k[l], "w_o_mla": w_o_mla[l], "w_o_dsa": w_o_dsa[l],
             "w_o_mem": w_o_mem[l], "w_out": w_out[l]}
        ffn1 = (g_ffn1[l], w_ffn1_gate[l], w_ffn1_up[l], w_ffn1_down[l])
        ffn2 = (g_ffn2[l], w_ffn2_gate[l], w_ffn2_up[l], w_ffn2_down[l])
        uv = w_mla_uv[l]
        mk_p, mv_p = memory_kv(mem_prompt, p)
        xp = ffn_half(xp, *ffn1)
        pr_p = project_mixers(rms_norm(xp, g_mix[l]), pos_p, p)
        xp = xp + merge_branches(*prompt_core(pr_p, mk_p, mv_p, uv), pr_p["gates"], p)
        xp = ffn_half(xp, *ffn2)
        xs = ffn_half(xs, *ffn1)
        pr_s = project_mixers(rms_norm(xs, g_mix[l]), pos_s, p)
        o_s = sample_core(pr_s, cache_mem_k[l], cache_mem_v[l], uv, l, cache_mla_ckv, cache_mla_krope,
                          cache_dsa_k, cache_dsa_v, cache_idx_k, page_table)
        xs = xs + merge_branches(*o_s, pr_s["gates"], p)
        xs = ffn_half(xs, *ffn2)
        for tag, pr in (("p", pr_p), ("s", pr_s)):
            st["ckv_" + tag].append(pr["ckv"])
            st["kr_" + tag].append(pr["krope"])
            st["dk_" + tag].append(pr["dsa_k"])
            st["dv_" + tag].append(pr["dsa_v"])
            st["ik_" + tag].append(pr["idx_k"])
        st["mk_p"].append(mk_p)
        st["mv_p"].append(mv_p)
    return (xp, xs,
            jnp.stack(st["ckv_p"]), jnp.stack(st["kr_p"]), jnp.stack(st["dk_p"]), jnp.stack(st["dv_p"]),
            jnp.stack(st["ik_p"]), jnp.stack(st["mk_p"]), jnp.stack(st["mv_p"]),
            jnp.stack(st["ckv_s"]), jnp.stack(st["kr_s"]), jnp.stack(st["dk_s"]), jnp.stack(st["dv_s"]),
            jnp.stack(st["ik_s"]))
```

```python
import functools

import jax
import jax.numpy as jnp
from jax import lax
from jax.experimental import pallas as pl
from jax.experimental.pallas import tpu as pltpu

F32, BF16, I32 = jnp.float32, jnp.bfloat16, jnp.int32

D_MODEL = 2048
D_FF = 5632
PAST_LEN = 8192
PAGE_SIZE = 128
ROPE_THETA = 500000.0
EPS = 1e-6
MLA_HEADS, MLA_D_NOPE, MLA_D_ROPE, MLA_D_V, MLA_D_CQ, MLA_D_CKV = 8, 128, 64, 128, 768, 512
MLA_D_QK = MLA_D_NOPE + MLA_D_ROPE
DSA_HEADS, DSA_KV_HEADS, DSA_HEAD_DIM = 8, 2, 128
DSA_GROUP = DSA_HEADS // DSA_KV_HEADS
DSA_ROT = DSA_HEAD_DIM // 4
IDX_HEADS, IDX_DIM = 16, 64
IDX_ROT = IDX_DIM // 4
TOPK_MAX = 256
MEM_HEADS, MEM_HEAD_DIM = 4, 256
N_BRANCH = 3

LANES = 128
SUBLANES = 8
NEG = -1e30
INT_MIN = -2 ** 31
VMEM_LIMIT = 56 * 1024 * 1024

MLA_SCALE = MLA_D_QK ** -0.5
DSA_SCALE = DSA_HEAD_DIM ** -0.5
MEM_SCALE = MEM_HEAD_DIM ** -0.5

C_CQ = 0
C_CKV = C_CQ + MLA_D_CQ
C_DQ = C_CKV + MLA_D_CKV
C_DK = C_DQ + DSA_HEADS * DSA_HEAD_DIM
C_DV = C_DK + DSA_KV_HEADS * DSA_HEAD_DIM
C_IQ = C_DV + DSA_KV_HEADS * DSA_HEAD_DIM
C_MQ = C_IQ + IDX_HEADS * IDX_DIM
C_KRI = C_MQ + MEM_HEADS * MEM_HEAD_DIM
C_IW = C_KRI + LANES
C_GATE = C_IW + LANES
RT_MLA, RT_DSA, RT_IDX, RT_KRI, RT_SLOTS = 0, 3, 6, 9, 14

NT_DIMS = (((1,), (1,)), ((), ()))


def _cparams(*sem):
    return pltpu.CompilerParams(dimension_semantics=sem, vmem_limit_bytes=VMEM_LIMIT)


def _rms(x, g):
    return x * lax.rsqrt(jnp.mean(x * x, axis=-1, keepdims=True) + EPS) * g


def _dot(a, b):
    return jnp.dot(a, b, preferred_element_type=F32)


def _dot_nt(a, b):
    return lax.dot_general(a, b, NT_DIMS, preferred_element_type=F32)


def _roll(x, s):
    return pltpu.roll(x, s % x.shape[-1], axis=x.ndim - 1)


def _softmax_update(s, v, m_ref, l_ref, acc_ref):
    m_prev = m_ref[...]
    m_new = jnp.maximum(m_prev, jnp.max(s, axis=1, keepdims=True))
    alpha = jnp.exp(m_prev - m_new)
    p = jnp.exp(s - m_new)
    l_ref[...] = alpha * l_ref[...] + jnp.sum(p, axis=1, keepdims=True)
    acc_ref[...] = alpha * acc_ref[...] + _dot(p.astype(BF16), v)
    m_ref[...] = m_new


def _sortable_key(x):
    bits = pltpu.bitcast(x + 0.0, I32)
    return bits ^ ((bits >> 31) & 0x7FFFFFFF)


def _ffn_kernel(x_ref, g_ref, wg_ref, wu_ref, wd_ref, o_ref, h_ref):
    @pl.when(pl.program_id(1) == 0)
    def _():
        x = x_ref[...]
        h_ref[...] = _rms(x, g_ref[...]).astype(BF16)
        o_ref[...] = x

    h = h_ref[...]
    a = _dot(h, wg_ref[...])
    b = _dot(h, wu_ref[...])
    act = (a * jax.nn.sigmoid(a) * b).astype(BF16)
    o_ref[...] += 0.5 * _dot(act, wd_ref[...])


def _ffn_half(x, g, wg, wu, wd):
    T, D = x.shape
    F = wg.shape[1]
    tm = min(T, 512)
    tf = min(F, 512)
    return pl.pallas_call(
        _ffn_kernel,
        grid=(T // tm, F // tf),
        in_specs=[
            pl.BlockSpec((tm, D), lambda i, j: (i, 0)),
            pl.BlockSpec((1, D), lambda i, j: (0, 0)),
            pl.BlockSpec((D, tf), lambda i, j: (0, j)),
            pl.BlockSpec((D, tf), lambda i, j: (0, j)),
            pl.BlockSpec((tf, D), lambda i, j: (j, 0)),
        ],
        out_specs=pl.BlockSpec((tm, D), lambda i, j: (i, 0)),
        out_shape=jax.ShapeDtypeStruct((T, D), F32),
        scratch_shapes=[pltpu.VMEM((tm, D), BF16)],
        compiler_params=_cparams("parallel", "arbitrary"),
    )(x, g, wg, wu, wd)


def _proj_kernel(x_ref, g_ref, w_ref, o_ref, h_ref):
    @pl.when(pl.program_id(1) == 0)
    def _():
        h_ref[...] = _rms(x_ref[...], g_ref[...]).astype(BF16)

    o_ref[...] = _dot(h_ref[...], w_ref[...])


def _project(x, g, w):
    T, D = x.shape
    N = w.shape[1]
    tm = min(T, 1024)
    tn = 512
    return pl.pallas_call(
        _proj_kernel,
        grid=(T // tm, N // tn),
        in_specs=[
            pl.BlockSpec((tm, D), lambda i, j: (i, 0)),
            pl.BlockSpec((1, D), lambda i, j: (0, 0)),
            pl.BlockSpec((D, tn), lambda i, j: (0, j)),
        ],
        out_specs=pl.BlockSpec((tm, tn), lambda i, j: (i, j)),
        out_shape=jax.ShapeDtypeStruct((T, N), F32),
        scratch_shapes=[pltpu.VMEM((tm, D), BF16)],
        compiler_params=_cparams("parallel", "arbitrary"),
    )(x, g, w)


def _rope3(x, rt_ref, slot, half):
    c = rt_ref[:, slot * LANES:(slot + 1) * LANES]
    s1 = rt_ref[:, (slot + 1) * LANES:(slot + 2) * LANES]
    s2 = rt_ref[:, (slot + 2) * LANES:(slot + 3) * LANES]
    return x * c + _roll(x, -half) * s1 + _roll(x, half) * s2


def _epilogue_kernel(c_ref, rt_ref, gcq_ref, gqn_ref, gqr_ref, gckv_ref, gkr_ref, gdq_ref, gdk_ref, gmq_ref,
                     wuq_ref, wx_ref, *outs, sample):
    if sample:
        ckv_o, kr_o, dk_o, dv_o, ik_o, iw_o, qlat_o, qr_o, dq_o, iq_o, mq_o = outs
    else:
        (ckv_o, kr_o, dk_o, dv_o, ik_o, iw_o, qcat_o, kcat_o, vb_o, dq_o, dkb_o, dvb_o, iq_o, ikb_o, mq_o) = outs
    tm = c_ref.shape[0]
    lo = lax.broadcasted_iota(I32, (tm, LANES), 1) < MLA_D_ROPE

    cqn = _rms(c_ref[:, C_CQ:C_CQ + MLA_D_CQ], gcq_ref[...]).astype(BF16)
    q = _dot(cqn, wuq_ref[...])
    rope0 = MLA_HEADS * MLA_D_NOPE
    for p in range(MLA_HEADS // 2):
        xr = q[:, rope0 + p * LANES: rope0 + (p + 1) * LANES]
        x2 = xr * xr
        ss_rope = (jnp.sum(jnp.where(lo, x2, 0.0), axis=1, keepdims=True),
                   jnp.sum(jnp.where(lo, 0.0, x2), axis=1, keepdims=True))
        inv = []
        for k in range(2):
            h = 2 * p + k
            qn = q[:, h * MLA_D_NOPE:(h + 1) * MLA_D_NOPE]
            ss = jnp.sum(qn * qn, axis=1, keepdims=True) + ss_rope[k]
            inv.append(lax.rsqrt(ss * (1.0 / MLA_D_QK) + EPS))
            qn = (qn * inv[k] * gqn_ref[...] * MLA_SCALE).astype(BF16)
            if sample:
                qlat_o[:, h * MLA_D_CKV:(h + 1) * MLA_D_CKV] = _dot(qn, wx_ref[h]).astype(BF16)
            else:
                qcat_o[:, h * 2 * LANES: h * 2 * LANES + LANES] = qn
        xr = xr * jnp.where(lo, inv[0], inv[1]) * gqr_ref[...] * MLA_SCALE
        xr = _rope3(xr, rt_ref, RT_MLA, MLA_D_ROPE // 2)
        if sample:
            qr_o[:, p * LANES:(p + 1) * LANES] = xr.astype(BF16)
        else:
            qcat_o[:, (4 * p + 1) * LANES:(4 * p + 2) * LANES] = jnp.where(lo, xr, 0.0).astype(BF16)
            qcat_o[:, (4 * p + 3) * LANES:(4 * p + 4) * LANES] = jnp.where(lo, _roll(xr, MLA_D_ROPE), 0.0).astype(BF16)

    ckv = _rms(c_ref[:, C_CKV:C_CKV + MLA_D_CKV], gckv_ref[...])
    ckv_o[...] = ckv
    x = c_ref[:, C_KRI:C_KRI + LANES]
    ssk = jnp.sum(jnp.where(lo, x * x, 0.0), axis=1, keepdims=True)
    x = jnp.where(lo, x * lax.rsqrt(ssk * (1.0 / MLA_D_ROPE) + EPS) * gkr_ref[...], x)
    s = RT_KRI
    kri = (x * rt_ref[:, s * LANES:(s + 1) * LANES]
           + _roll(x, -(MLA_D_ROPE // 2)) * rt_ref[:, (s + 1) * LANES:(s + 2) * LANES]
           + _roll(x, MLA_D_ROPE // 2) * rt_ref[:, (s + 2) * LANES:(s + 3) * LANES]
           + _roll(x, -(IDX_ROT // 2)) * rt_ref[:, (s + 3) * LANES:(s + 4) * LANES]
           + _roll(x, IDX_ROT // 2) * rt_ref[:, (s + 4) * LANES:(s + 5) * LANES])
    kr_o[...] = kri[:, :MLA_D_ROPE]
    ik_o[...] = kri[:, MLA_D_ROPE:]
    if not sample:
        ikb_o[...] = kri[:, MLA_D_ROPE:].astype(BF16)
        kv = _dot(ckv.astype(BF16), wx_ref[...])
        vb_o[...] = kv[:, MLA_HEADS * MLA_D_NOPE:].astype(BF16)
        krb = jnp.where(lo, kri, 0.0).astype(BF16)
        for h in range(MLA_HEADS):
            kcat_o[:, h * 2 * LANES: h * 2 * LANES + LANES] = kv[:, h * MLA_D_NOPE:(h + 1) * MLA_D_NOPE].astype(BF16)
            kcat_o[:, h * 2 * LANES + LANES:(h + 1) * 2 * LANES] = krb

    for h in range(DSA_HEADS):
        x = _rms(c_ref[:, C_DQ + h * LANES: C_DQ + (h + 1) * LANES], gdq_ref[...])
        dq_o[:, h * LANES:(h + 1) * LANES] = (_rope3(x, rt_ref, RT_DSA, DSA_ROT // 2) * DSA_SCALE).astype(BF16)
    for g in range(DSA_KV_HEADS):
        x = _rms(c_ref[:, C_DK + g * LANES: C_DK + (g + 1) * LANES], gdk_ref[...])
        x = _rope3(x, rt_ref, RT_DSA, DSA_ROT // 2)
        dk_o[:, g * LANES:(g + 1) * LANES] = x
        if not sample:
            dkb_o[:, g * LANES:(g + 1) * LANES] = x.astype(BF16)
    x = c_ref[:, C_DV:C_DV + DSA_KV_HEADS * DSA_HEAD_DIM]
    dv_o[...] = x
    if not sample:
        dvb_o[...] = x.astype(BF16)

    for p in range(IDX_HEADS // 2):
        x = c_ref[:, C_IQ + p * LANES: C_IQ + (p + 1) * LANES]
        iq_o[:, p * LANES:(p + 1) * LANES] = _rope3(x, rt_ref, RT_IDX, IDX_ROT // 2).astype(BF16)
    iw_o[...] = c_ref[:, C_IW:C_IW + LANES]

    for h in range(MEM_HEADS):
        x = _rms(c_ref[:, C_MQ + h * MEM_HEAD_DIM: C_MQ + (h + 1) * MEM_HEAD_DIM], gmq_ref[...])
        mq_o[:, h * MEM_HEAD_DIM:(h + 1) * MEM_HEAD_DIM] = (x * MEM_SCALE).astype(BF16)


def _epilogue(c, rt, gains, wuq, wx, sample):
    T = c.shape[0]
    tm = min(T, 256)
    n_rt = rt.shape[0] // tm
    row = lambda n: pl.BlockSpec((tm, n), lambda i: (i, 0))
    full = lambda a: pl.BlockSpec(a.shape, lambda i: (0,) * a.ndim)
    widths_f32 = [MLA_D_CKV, MLA_D_ROPE, 2 * LANES, 2 * LANES, IDX_DIM, LANES]
    if sample:
        widths_b16 = [MLA_HEADS * MLA_D_CKV, MLA_HEADS * MLA_D_ROPE, 8 * LANES, 8 * LANES, 8 * LANES]
    else:
        widths_b16 = [16 * LANES, 16 * LANES, 8 * LANES, 8 * LANES, 2 * LANES, 2 * LANES, 8 * LANES, IDX_DIM, 8 * LANES]
    out_shape = ([jax.ShapeDtypeStruct((T, n), F32) for n in widths_f32]
                 + [jax.ShapeDtypeStruct((T, n), BF16) for n in widths_b16])
    return pl.pallas_call(
        functools.partial(_epilogue_kernel, sample=sample),
        grid=(T // tm,),
        in_specs=[pl.BlockSpec((tm, C_GATE), lambda i: (i, 0)),
                  pl.BlockSpec((tm, RT_SLOTS * LANES), lambda i: (i % n_rt, 0))]
                 + [full(g) for g in gains] + [full(wuq), full(wx)],
        out_specs=[row(n) for n in widths_f32 + widths_b16],
        out_shape=out_shape,
        compiler_params=_cparams("parallel"),
    )(c, rt, *gains, wuq, wx)


def _mla_flash_kernel(q_ref, k_ref, v_ref, o_ref, m_ref, l_ref, acc_ref):
    i, j = pl.program_id(2), pl.program_id(3)
    tq, tk = q_ref.shape[0], k_ref.shape[0]

    @pl.when(j == 0)
    def _():
        m_ref[...] = jnp.full(m_ref.shape, NEG, F32)
        l_ref[...] = jnp.zeros(l_ref.shape, F32)
        acc_ref[...] = jnp.zeros(acc_ref.shape, F32)

    @pl.when(j < i)
    def _():
        _softmax_update(_dot_nt(q_ref[...], k_ref[...]), v_ref[...], m_ref, l_ref, acc_ref)

    @pl.when(j == i)
    def _():
        s = _dot_nt(q_ref[...], k_ref[...])
        row = lax.broadcasted_iota(I32, (tq, tk), 0)
        col = lax.broadcasted_iota(I32, (tq, tk), 1)
        _softmax_update(jnp.where(col <= row, s, NEG), v_ref[...], m_ref, l_ref, acc_ref)
        o_ref[...] = (acc_ref[...] / l_ref[...]).astype(o_ref.dtype)


def _mla_prompt(qcat, kcat, vb, B, T):
    tq = min(T, 512)
    nq = T // tq
    H = MLA_HEADS
    return pl.pallas_call(
        _mla_flash_kernel,
        grid=(B, H, nq, nq),
        in_specs=[
            pl.BlockSpec((tq, 2 * LANES), lambda b, h, i, j: (b * nq + i, h)),
            pl.BlockSpec((tq, 2 * LANES), lambda b, h, i, j: (b * nq + jnp.minimum(i, j), h)),
            pl.BlockSpec((tq, MLA_D_V), lambda b, h, i, j: (b * nq + jnp.minimum(i, j), h)),
        ],
        out_specs=pl.BlockSpec((tq, MLA_D_V), lambda b, h, i, j: (b * nq + i, h)),
        out_shape=jax.ShapeDtypeStruct((B * T, H * MLA_D_V), BF16),
        scratch_shapes=[pltpu.VMEM((tq, 1), F32), pltpu.VMEM((tq, 1), F32), pltpu.VMEM((tq, MLA_D_V), F32)],
        compiler_params=_cparams("parallel", "parallel", "parallel", "arbitrary"),
    )(qcat, kcat, vb)


def _select_threshold(count_fn, shape, ksel, n_index_bits):
    def key_step(s, prefix):
        cand = prefix | lax.shift_left(jnp.int32(1), 31 - s)
        cand_s = cand ^ INT_MIN
        n = count_fn(lambda k, col: k >= cand_s)
        return jnp.where(n >= ksel, cand, prefix)

    thr = lax.fori_loop(0, 32, key_step, jnp.zeros(shape, I32)) ^ INT_MIN
    need = ksel - count_fn(lambda k, col: k > thr)

    def cut_step(s, cut):
        cand = cut | lax.shift_left(jnp.int32(1), n_index_bits - 1 - s)
        n = count_fn(lambda k, col: (k == thr) & (col < cand))
        return jnp.where(n <= need, cand, cut)

    cut = lax.fori_loop(0, n_index_bits, cut_step, jnp.zeros(shape, I32))
    return thr, cut


def _dsa_prompt_kernel(iq_ref, iw_ref, ik_ref, dq_ref, dk_ref, dv_ref, o_ref,
                       key_ref, bias_ref, m_ref, l_ref, acc_ref, *, ck, ksel):
    i = pl.program_id(1)
    tq = dq_ref.shape[0]
    T = ik_ref.shape[0]
    nch = ((i + 1) * tq + ck - 1) // ck
    row = i * tq + lax.broadcasted_iota(I32, (tq, ck), 0)
    col0 = lax.broadcasted_iota(I32, (tq, ck), 1)

    def idx_chunk(c, carry):
        ikc = ik_ref[pl.ds(pl.multiple_of(c * ck, ck), ck), :]
        acc = jnp.zeros((tq, ck), F32)
        for h in range(IDX_HEADS):
            acc = acc + jnp.maximum(_dot_nt(iq_ref[h], ikc), 0.0) * iw_ref[:, h:h + 1]
        key_ref[c] = jnp.where(c * ck + col0 <= row, _sortable_key(acc), INT_MIN)
        return carry

    lax.fori_loop(0, nch, idx_chunk, 0)

    def count_fn(pred):
        def body(c, a):
            hit = jnp.where(pred(key_ref[c], c * ck + col0), 1.0, 0.0)
            for t in range(ck // LANES):
                a = a + hit[:, t * LANES:(t + 1) * LANES]
            return a
        a = lax.fori_loop(0, nch, body, jnp.zeros((tq, LANES), F32))
        return jnp.sum(a, axis=1, keepdims=True)

    thr, cut = _select_threshold(count_fn, (tq, 1), ksel, T.bit_length())

    def bias_chunk(c, carry):
        k = key_ref[c]
        col = c * ck + col0
        sel = (col <= row) & ((k > thr) | ((k == thr) & (col < cut)))
        bias_ref[c] = jnp.where(sel, 0.0, NEG)
        return carry

    lax.fori_loop(0, nch, bias_chunk, 0)

    for g in range(DSA_KV_HEADS):
        qg = jnp.concatenate([dq_ref[:, (g * DSA_GROUP + r) * LANES:(g * DSA_GROUP + r + 1) * LANES]
                              for r in range(DSA_GROUP)], axis=0)
        m_ref[...] = jnp.full(m_ref.shape, NEG, F32)
        l_ref[...] = jnp.zeros(l_ref.shape, F32)
        acc_ref[...] = jnp.zeros(acc_ref.shape, F32)

        def attn_chunk(c, carry):
            k0 = pl.multiple_of(c * ck, ck)
            kc = dk_ref[pl.ds(k0, ck), g * LANES:(g + 1) * LANES]
            vc = dv_ref[pl.ds(k0, ck), g * LANES:(g + 1) * LANES]
            b = bias_ref[c]
            s = _dot_nt(qg, kc) + jnp.concatenate([b] * DSA_GROUP, axis=0)
            _softmax_update(s, vc, m_ref, l_ref, acc_ref)
            return carry

        lax.fori_loop(0, nch, attn_chunk, 0)
        o = acc_ref[...] / l_ref[...]
        for r in range(DSA_GROUP):
            o_ref[:, (g * DSA_GROUP + r) * LANES:(g * DSA_GROUP + r + 1) * LANES] = (
                o[r * tq:(r + 1) * tq].astype(o_ref.dtype))


def _dsa_prompt(iq3, iw, ikb, dq, dkb, dvb, B, T):
    tq = min(T, 256)
    ck = min(T, 512)
    nq = T // tq
    ksel = min(TOPK_MAX, T // 4)
    return pl.pallas_call(
        functools.partial(_dsa_prompt_kernel, ck=ck, ksel=ksel),
        grid=(B, nq),
        in_specs=[
            pl.BlockSpec((IDX_HEADS, tq, IDX_DIM), lambda b, i: (0, b * nq + i, 0)),
            pl.BlockSpec((tq, LANES), lambda b, i: (b * nq + i, 0)),
            pl.BlockSpec((T, IDX_DIM), lambda b, i: (b, 0)),
            pl.BlockSpec((tq, DSA_HEADS * LANES), lambda b, i: (b * nq + i, 0)),
            pl.BlockSpec((T, DSA_KV_HEADS * LANES), lambda b, i: (b, 0)),
            pl.BlockSpec((T, DSA_KV_HEADS * LANES), lambda b, i: (b, 0)),
        ],
        out_specs=pl.BlockSpec((tq, DSA_HEADS * LANES), lambda b, i: (b * nq + i, 0)),
        out_shape=jax.ShapeDtypeStruct((B * T, DSA_HEADS * LANES), BF16),
        scratch_shapes=[pltpu.VMEM((T // ck, tq, ck), I32), pltpu.VMEM((T // ck, tq, ck), F32),
                        pltpu.VMEM((DSA_GROUP * tq, 1), F32), pltpu.VMEM((DSA_GROUP * tq, 1), F32),
                        pltpu.VMEM((DSA_GROUP * tq, LANES), F32)],
        compiler_params=_cparams("parallel", "arbitrary"),
    )(iq3, iw, ikb, dq, dkb, dvb)


def _memkv_kernel(x_ref, g_ref, w_ref, gk_ref, k_ref, v_ref):
    h = _rms(x_ref[...], g_ref[...]).astype(BF16)
    kv = _dot(h, w_ref[...])
    n = MEM_HEADS * MEM_HEAD_DIM
    for hd in range(MEM_HEADS):
        k_ref[:, hd * MEM_HEAD_DIM:(hd + 1) * MEM_HEAD_DIM] = _rms(
            kv[:, hd * MEM_HEAD_DIM:(hd + 1) * MEM_HEAD_DIM], gk_ref[...])
    v_ref[...] = kv[:, n:]


def _memory_kv(mem, g_in, w, g_k):
    T, D = mem.shape
    n = MEM_HEADS * MEM_HEAD_DIM
    tm = min(T, 256)
    return pl.pallas_call(
        _memkv_kernel,
        grid=(T // tm,),
        in_specs=[pl.BlockSpec((tm, D), lambda i: (i, 0)), pl.BlockSpec((1, D), lambda i: (0, 0)),
                  pl.BlockSpec((D, 2 * n), lambda i: (0, 0)), pl.BlockSpec((1, MEM_HEAD_DIM), lambda i: (0, 0))],
        out_specs=[pl.BlockSpec((tm, n), lambda i: (i, 0))] * 2,
        out_shape=[jax.ShapeDtypeStruct((T, n), F32)] * 2,
        compiler_params=_cparams("parallel"),
    )(mem, g_in, w, g_k)


def _mem_attend_kernel(q_ref, k_ref, v_ref, o_ref):
    for h in range(MEM_HEADS):
        sl = slice(h * MEM_HEAD_DIM, (h + 1) * MEM_HEAD_DIM)
        s = _dot_nt(q_ref[:, sl], k_ref[:, sl].astype(BF16))
        p = jnp.exp(s - jnp.max(s, axis=1, keepdims=True))
        p = p / jnp.sum(p, axis=1, keepdims=True)
        o_ref[:, sl] = _dot(p.astype(BF16), v_ref[:, sl].astype(BF16)).astype(o_ref.dtype)


def _mem_attend(mq, mk, mv):
    G, R, n = mq.shape
    M = mk.shape[1]
    tm = min(R, 512)
    return pl.pallas_call(
        _mem_attend_kernel,
        grid=(G, R // tm),
        in_specs=[pl.BlockSpec((None, tm, n), lambda g, i: (g, i, 0)),
                  pl.BlockSpec((None, M, n), lambda g, i: (g, 0, 0)),
                  pl.BlockSpec((None, M, n), lambda g, i: (g, 0, 0))],
        out_specs=pl.BlockSpec((None, tm, n), lambda g, i: (g, i, 0)),
        out_shape=jax.ShapeDtypeStruct((G, R, n), BF16),
        compiler_params=_cparams("parallel", "parallel"),
    )(mq, mk, mv)


def _merge_kernel(x_ref, oa_ref, ob_ref, oc_ref, ga_ref, gb_ref, gc_ref, wa_ref, wb_ref, wc_ref, wo_ref, o_ref):
    @pl.when(pl.program_id(1) == 0)
    def _():
        o_ref[...] = x_ref[...]

    merged = (jax.nn.sigmoid(ga_ref[...]) * _dot(oa_ref[...], wa_ref[...])
              + jax.nn.sigmoid(gb_ref[...]) * _dot(ob_ref[...], wb_ref[...])
              + jax.nn.sigmoid(gc_ref[...]) * _dot(oc_ref[...], wc_ref[...]))
    o_ref[...] += _dot(merged.astype(BF16), wo_ref[...])


def _merge(x, oa, ob, oc, c, wa, wb, wc, wo):
    T, D = x.shape
    n = oa.shape[1]
    tm = min(T, 512)
    tn = min(D, 512)
    g0 = C_GATE // tn
    nb = D // tn
    gate = lambda k: pl.BlockSpec((tm, tn), lambda i, j: (i, g0 + k * nb + j))
    branch_w = pl.BlockSpec((n, tn), lambda i, j: (0, j))
    rows = pl.BlockSpec((tm, n), lambda i, j: (i, 0))
    return pl.pallas_call(
        _merge_kernel,
        grid=(T // tm, nb),
        in_specs=[pl.BlockSpec((tm, D), lambda i, j: (i, 0)), rows, rows, rows, gate(0), gate(1), gate(2),
                  branch_w, branch_w, branch_w, pl.BlockSpec((tn, D), lambda i, j: (j, 0))],
        out_specs=pl.BlockSpec((tm, D), lambda i, j: (i, 0)),
        out_shape=jax.ShapeDtypeStruct((T, D), F32),
        compiler_params=_cparams("parallel", "arbitrary"),
    )(x, oa, ob, oc, c, c, c, wa, wb, wc, wo)


def _sample_index_kernel(pt_ref, iq_ref, iw_ref, ikn_ref, *rest, pages, n_steps, ksel):
    page_refs = rest[:pages]
    bias_ref, key_ref = rest[pages:]
    j = pl.program_id(1)
    R = iq_ref.shape[0]
    TP = R // IDX_HEADS
    past = n_steps * pages * PAGE_SIZE
    iq = iq_ref[...]
    iw = iw_ref[...]

    def scores(ik):
        s = jnp.maximum(_dot_nt(iq, ik.astype(BF16)), 0.0) * iw
        return jnp.sum(s.reshape(TP, IDX_HEADS, PAGE_SIZE), axis=1)

    for k in range(pages):
        key_ref[j * pages + k] = _sortable_key(scores(page_refs[k][...]))

    @pl.when(j == n_steps - 1)
    def _():
        t = lax.broadcasted_iota(I32, (TP, PAGE_SIZE), 0)
        cn = lax.broadcasted_iota(I32, (TP, PAGE_SIZE), 1)
        key_ref[n_steps * pages] = jnp.where(cn <= t, _sortable_key(scores(ikn_ref[...])), INT_MIN)
        keys = key_ref[...]
        col = (lax.broadcasted_iota(I32, keys.shape, 0) * PAGE_SIZE
               + lax.broadcasted_iota(I32, keys.shape, 2))

        def count_fn(pred):
            hit = jnp.where(pred(keys, col), 1.0, 0.0)
            return jnp.sum(jnp.sum(hit, axis=0), axis=1, keepdims=True)[None]

        thr, cut = _select_threshold(count_fn, (1, TP, 1), ksel, (past + PAGE_SIZE).bit_length())
        tt = lax.broadcasted_iota(I32, keys.shape, 1)
        visible = col <= past + tt
        sel = visible & ((keys > thr) | ((keys == thr) & (col < cut)))
        bias_ref[...] = jnp.where(sel, 0.0, NEG)


def _sample_index(pt, iq, iw, ik_new, cache_ik, n_pages, ksel):
    DB, R, _ = iq.shape
    pages = min(n_pages, 16)
    n_steps = n_pages // pages
    TP = R // IDX_HEADS

    def page_spec(k):
        return pl.BlockSpec((None, PAGE_SIZE, IDX_DIM),
                            lambda b, j, pt: (pt[b * n_pages + j * pages + k], 0, 0))

    grid_spec = pltpu.PrefetchScalarGridSpec(
        num_scalar_prefetch=1,
        grid=(DB, n_steps),
        in_specs=[pl.BlockSpec((None, R, IDX_DIM), lambda b, j, pt: (b, 0, 0)),
                  pl.BlockSpec((None, R, 1), lambda b, j, pt: (b, 0, 0)),
                  pl.BlockSpec((None, PAGE_SIZE, IDX_DIM), lambda b, j, pt: (b, 0, 0))]
                 + [page_spec(k) for k in range(pages)],
        out_specs=pl.BlockSpec((None, n_pages + 1, TP, PAGE_SIZE), lambda b, j, pt: (b, 0, 0, 0)),
        scratch_shapes=[pltpu.VMEM((n_pages + 1, TP, PAGE_SIZE), I32)],
    )
    return pl.pallas_call(
        functools.partial(_sample_index_kernel, pages=pages, n_steps=n_steps, ksel=ksel),
        grid_spec=grid_spec,
        out_shape=jax.ShapeDtypeStruct((DB, n_pages + 1, TP, PAGE_SIZE), F32),
        compiler_params=_cparams("parallel", "arbitrary"),
    )(pt, iq, iw, ik_new, *([cache_ik] * pages))


def _sample_attend_kernel(pt_ref, qlat_ref, qr_ref, dq_ref, bias_ref, biasn_ref, ckvn_ref, krn_ref, dkn_ref, dvn_ref, *rest,
                          pages, n_steps):
    ckv_pages, kr_pages = rest[:pages], rest[pages:2 * pages]
    dk_pages, dv_pages = rest[2 * pages:3 * pages], rest[3 * pages:4 * pages]
    (olat_ref, odsa_ref, ma_ref, la_ref, acca_ref, md_ref, ld_ref, accd_ref,
     ckv_cat, kr_cat, dk_cat, dv_cat) = rest[4 * pages:]
    j = pl.program_id(1)
    RA = qlat_ref.shape[0]

    @pl.when(j == 0)
    def _():
        ma_ref[...] = jnp.full(ma_ref.shape, NEG, F32)
        la_ref[...] = jnp.zeros(la_ref.shape, F32)
        acca_ref[...] = jnp.zeros(acca_ref.shape, F32)
        md_ref[...] = jnp.full(md_ref.shape, NEG, F32)
        ld_ref[...] = jnp.zeros(ld_ref.shape, F32)
        accd_ref[...] = jnp.zeros(accd_ref.shape, F32)

    for k in range(pages):
        sl = slice(k * PAGE_SIZE, (k + 1) * PAGE_SIZE)
        ckv_cat[sl, :] = ckv_pages[k][...].astype(BF16)
        kr_cat[sl, :] = kr_pages[k][...].astype(BF16)
        dk_cat[sl, :] = dk_pages[k][...].astype(BF16)
        dv_cat[sl, :] = dv_pages[k][...].astype(BF16)

    def attend(ckv, kr, dk, dv, bias, mask_a):
        s = _dot_nt(qlat_ref[...], ckv) + _dot_nt(qr_ref[...], kr)
        if mask_a is not None:
            s = jnp.where(mask_a, s, NEG)
        _softmax_update(s, ckv, ma_ref, la_ref, acca_ref)
        bias4 = jnp.concatenate([bias] * DSA_GROUP, axis=0)
        for g in range(DSA_KV_HEADS):
            sd = _dot_nt(dq_ref[g], dk[:, g * LANES:(g + 1) * LANES]) + bias4
            _softmax_update(sd, dv[:, g * LANES:(g + 1) * LANES], md_ref.at[g], ld_ref.at[g], accd_ref.at[g])

    bias = jnp.concatenate([bias_ref[k] for k in range(pages)], axis=1)
    attend(ckv_cat[...], kr_cat[...], dk_cat[...], dv_cat[...], bias, None)

    @pl.when(j == n_steps - 1)
    def _():
        t = lax.broadcasted_iota(I32, (RA, PAGE_SIZE), 0) // MLA_HEADS
        cn = lax.broadcasted_iota(I32, (RA, PAGE_SIZE), 1)
        attend(ckvn_ref[...].astype(BF16), krn_ref[...].astype(BF16), dkn_ref[...].astype(BF16),
               dvn_ref[...].astype(BF16), biasn_ref[0], cn <= t)
        olat_ref[...] = acca_ref[...] / la_ref[...]
        for g in range(DSA_KV_HEADS):
            odsa_ref[g] = accd_ref[g] / ld_ref[g]


def _sample_attend(pt, qlat, qr, dq, bias, new_rows, caches, n_pages):
    DB, RA, _ = qlat.shape
    RD = dq.shape[2]
    TP = bias.shape[2]
    pages = min(n_pages, 8)
    n_steps = n_pages // pages
    widths = [MLA_D_CKV, MLA_D_ROPE, 2 * LANES, 2 * LANES]

    def page_spec(k, w):
        return pl.BlockSpec((None, PAGE_SIZE, w), lambda b, j, pt: (pt[b * n_pages + j * pages + k], 0, 0))

    per_seq = lambda shape: pl.BlockSpec((None,) + shape, lambda b, j, pt: (b,) + (0,) * len(shape))
    in_specs = [per_seq((RA, MLA_D_CKV)), per_seq((RA, MLA_D_ROPE)), per_seq((DSA_KV_HEADS, RD, LANES)),
                pl.BlockSpec((None, pages, TP, PAGE_SIZE), lambda b, j, pt: (b, j, 0, 0)),
                pl.BlockSpec((None, 1, TP, PAGE_SIZE), lambda b, j, pt: (b, n_pages, 0, 0))]
    in_specs += [per_seq((PAGE_SIZE, w)) for w in widths]
    for w in widths:
        in_specs += [page_spec(k, w) for k in range(pages)]
    grid_spec = pltpu.PrefetchScalarGridSpec(
        num_scalar_prefetch=1,
        grid=(DB, n_steps),
        in_specs=in_specs,
        out_specs=[per_seq((RA, MLA_D_CKV)), per_seq((DSA_KV_HEADS, RD, LANES))],
        scratch_shapes=[pltpu.VMEM((RA, 1), F32), pltpu.VMEM((RA, 1), F32), pltpu.VMEM((RA, MLA_D_CKV), F32),
                        pltpu.VMEM((DSA_KV_HEADS, RD, 1), F32), pltpu.VMEM((DSA_KV_HEADS, RD, 1), F32),
                        pltpu.VMEM((DSA_KV_HEADS, RD, LANES), F32),
                        pltpu.VMEM((pages * PAGE_SIZE, MLA_D_CKV), BF16),
                        pltpu.VMEM((pages * PAGE_SIZE, MLA_D_ROPE), BF16),
                        pltpu.VMEM((pages * PAGE_SIZE, 2 * LANES), BF16),
                        pltpu.VMEM((pages * PAGE_SIZE, 2 * LANES), BF16)],
    )
    operands = [qlat, qr, dq, bias, bias, *new_rows]
    for c in caches:
        operands += [c] * pages
    return pl.pallas_call(
        functools.partial(_sample_attend_kernel, pages=pages, n_steps=n_steps),
        grid_spec=grid_spec,
        out_shape=[jax.ShapeDtypeStruct((DB, RA, MLA_D_CKV), F32),
                   jax.ShapeDtypeStruct((DB, DSA_KV_HEADS, RD, LANES), F32)],
        compiler_params=_cparams("parallel", "arbitrary"),
    )(pt, *operands)


def _uv_kernel(o_ref, w_ref, out_ref):
    out_ref[...] = _dot(o_ref[...], w_ref[...]).astype(out_ref.dtype)


def _mla_up_v(olat_h, wuv_t):
    H, T, C = olat_h.shape
    return pl.pallas_call(
        _uv_kernel,
        grid=(H,),
        in_specs=[pl.BlockSpec((None, T, C), lambda h: (h, 0, 0)), pl.BlockSpec((None, C, MLA_D_V), lambda h: (h, 0, 0))],
        out_specs=pl.BlockSpec((T, MLA_D_V), lambda h: (0, h)),
        out_shape=jax.ShapeDtypeStruct((T, H * MLA_D_V), BF16),
        compiler_params=_cparams("parallel"),
    )(olat_h, wuv_t)


def _rope_tables(pos):
    pos = pos.astype(F32)
    T = pos.shape[0]

    def tabs(rot, period):
        half = rot // 2
        inv_freq = ROPE_THETA ** (-jnp.arange(half, dtype=F32) / half)
        ang = pos[:, None] * inv_freq[None, :]
        cos, sin = jnp.cos(ang), jnp.sin(ang)
        one, zero = jnp.ones((T, period - rot), F32), jnp.zeros((T, period - rot), F32)
        zh = jnp.zeros((T, half), F32)
        return (jnp.concatenate([cos, cos, one], 1), jnp.concatenate([-sin, zh, zero], 1),
                jnp.concatenate([zh, sin, zero], 1))

    rep = lambda ts, period: [jnp.tile(t, (1, LANES // period)) for t in ts]
    mla, idx = tabs(MLA_D_ROPE, MLA_D_ROPE), tabs(IDX_ROT, IDX_DIM)
    z = jnp.zeros((T, LANES // 2), F32)
    kri = [jnp.concatenate([mla[0], idx[0]], 1), jnp.concatenate([mla[1], z], 1), jnp.concatenate([mla[2], z], 1),
           jnp.concatenate([z, idx[1]], 1), jnp.concatenate([z, idx[2]], 1)]
    return jnp.concatenate(rep(mla, MLA_D_ROPE) + rep(tabs(DSA_ROT, LANES), LANES) + rep(idx, IDX_DIM) + kri, axis=1)


def _permute_w_in(w_in):
    offs, off = {}, 0
    for name, w in (("mla_cq", MLA_D_CQ), ("mla_ckv", MLA_D_CKV), ("mla_kr", MLA_D_ROPE),
                    ("dsa_q", DSA_HEADS * DSA_HEAD_DIM), ("dsa_k", DSA_KV_HEADS * DSA_HEAD_DIM),
                    ("dsa_v", DSA_KV_HEADS * DSA_HEAD_DIM), ("idx_q", IDX_HEADS * IDX_DIM), ("idx_w", IDX_HEADS),
                    ("idx_k", IDX_DIM), ("mem_q", MEM_HEADS * MEM_HEAD_DIM), ("gates", N_BRANCH * D_MODEL)):
        offs[name] = (off, w)
        off += w
    col = lambda n: w_in[:, offs[n][0]: offs[n][0] + offs[n][1]]
    pad = jnp.zeros((w_in.shape[0], LANES - IDX_HEADS), w_in.dtype)
    return jnp.concatenate([col("mla_cq"), col("mla_ckv"), col("dsa_q"), col("dsa_k"), col("dsa_v"), col("idx_q"),
                            col("mem_q"), col("mla_kr"), col("idx_k"), col("idx_w"), pad, col("gates")], axis=1)


def kernel(x_prompt, x_sample, cache_mla_ckv, cache_mla_krope, cache_dsa_k, cache_dsa_v, cache_idx_k, cache_mem_k,
           cache_mem_v, page_table, mem_prompt, g_ffn1, w_ffn1_gate, w_ffn1_up, w_ffn1_down, g_mix, w_in, g_mla_cq,
           w_mla_uq, g_mla_q, w_mla_uk, w_mla_uv, g_mla_ckv, g_mla_kr, g_dsa_q, g_dsa_k, g_mem_in, w_mem_kv,
           g_mem_q, g_mem_k, w_o_mla, w_o_dsa, w_o_mem, w_out, g_ffn2, w_ffn2_gate, w_ffn2_up, w_ffn2_down):
    B, S, D = x_prompt.shape
    DB, TS, _ = x_sample.shape
    depth = g_mix.shape[0]
    n_pages = page_table.shape[1]
    past = n_pages * PAGE_SIZE
    TP = SUBLANES
    xp = x_prompt.reshape(B * S, D)
    xs = x_sample.reshape(DB * TS, D)
    pt = page_table.reshape(-1)
    rt_p = _rope_tables(jnp.arange(S))
    rt_s = jnp.tile(_rope_tables(past + jnp.arange(TS)), (DB, 1))
    row = lambda g: g.reshape(1, -1).astype(F32)
    b16 = lambda w: w.astype(BF16)
    st = {k: [] for k in ("ckv_p", "kr_p", "dk_p", "dv_p", "ik_p", "mk_p", "mv_p", "ckv_s", "kr_s", "dk_s", "dv_s", "ik_s")}

    for l in range(depth):
        ffn1 = (row(g_ffn1[l]), b16(w_ffn1_gate[l]), b16(w_ffn1_up[l]), b16(w_ffn1_down[l]))
        ffn2 = (row(g_ffn2[l]), b16(w_ffn2_gate[l]), b16(w_ffn2_up[l]), b16(w_ffn2_down[l]))
        w_in_p = b16(_permute_w_in(w_in[l]))
        gq = g_mla_q[l]
        gains = (row(g_mla_cq[l]), row(gq[:MLA_D_NOPE]), row(jnp.tile(gq[MLA_D_NOPE:], 2)), row(g_mla_ckv[l]),
                 row(jnp.concatenate([g_mla_kr[l], jnp.ones((LANES - MLA_D_ROPE,), F32)])),
                 row(g_dsa_q[l]), row(g_dsa_k[l]), row(g_mem_q[l]))
        wuq = w_mla_uq[l].reshape(MLA_D_CQ, MLA_HEADS, MLA_D_QK)
        wuq = b16(jnp.concatenate([wuq[:, :, :MLA_D_NOPE].reshape(MLA_D_CQ, -1),
                                   wuq[:, :, MLA_D_NOPE:].reshape(MLA_D_CQ, -1)], axis=1))
        wuk, wuv = w_mla_uk[l], w_mla_uv[l]
        w_kv = b16(jnp.concatenate([wuk.transpose(2, 0, 1).reshape(MLA_D_CKV, -1),
                                    wuv.transpose(2, 0, 1).reshape(MLA_D_CKV, -1)], axis=1))
        merge_w = (b16(w_o_mla[l]), b16(w_o_dsa[l]), b16(w_o_mem[l]), b16(w_out[l]))

        mk_p, mv_p = _memory_kv(mem_prompt.reshape(-1, D), row(g_mem_in[l]), b16(w_mem_kv[l]), row(g_mem_k[l]))
        M = mem_prompt.shape[1]
        xp = _ffn_half(xp, *ffn1)
        c = _project(xp, row(g_mix[l]), w_in_p)
        (ckv, kr, dk, dv, ik, iw, qcat, kcat, vb, dq, dkb, dvb, iq, ikb, mq) = _epilogue(
            c, rt_p, gains, wuq, w_kv, sample=False)
        o_mla = _mla_prompt(qcat, kcat, vb, B, S)
        iq3 = iq.reshape(B * S, IDX_HEADS, IDX_DIM).transpose(1, 0, 2)
        o_dsa = _dsa_prompt(iq3, iw, ikb, dq, dkb, dvb, B, S)
        o_mem = _mem_attend(mq.reshape(B, S, -1), mk_p.reshape(B, M, -1), mv_p.reshape(B, M, -1)).reshape(B * S, -1)
        xp = _merge(xp, o_mla, o_dsa, o_mem, c, *merge_w)
        xp = _ffn_half(xp, *ffn2)
        st["ckv_p"].append(ckv.reshape(B, S, MLA_D_CKV))
        st["kr_p"].append(kr.reshape(B, S, MLA_D_ROPE))
        st["dk_p"].append(dk.reshape(B, S, DSA_KV_HEADS, DSA_HEAD_DIM))
        st["dv_p"].append(dv.reshape(B, S, DSA_KV_HEADS, DSA_HEAD_DIM))
        st["ik_p"].append(ik.reshape(B, S, IDX_DIM))
        st["mk_p"].append(mk_p.reshape(B, M, MEM_HEADS, MEM_HEAD_DIM))
        st["mv_p"].append(mv_p.reshape(B, M, MEM_HEADS, MEM_HEAD_DIM))

        xs = _ffn_half(xs, *ffn1)
        c = _project(xs, row(g_mix[l]), w_in_p)
        (ckv, kr, dk, dv, ik, iw, qlat, qr, dq, iq, mq) = _epilogue(c, rt_s, gains, wuq, b16(wuk), sample=True)
        pad_t = lambda a: jnp.pad(a.reshape(DB, TS, -1), ((0, 0), (0, PAGE_SIZE - TS), (0, 0)))
        iq_s = jnp.pad(iq.reshape(DB, TS, IDX_HEADS, IDX_DIM), ((0, 0), (0, TP - TS), (0, 0), (0, 0)))
        iw_s = jnp.pad(iw[:, :IDX_HEADS].reshape(DB, TS, IDX_HEADS), ((0, 0), (0, TP - TS), (0, 0)))
        ksel = min(TOPK_MAX, (past + TS) // 4)
        bias = _sample_index(pt, iq_s.reshape(DB, TP * IDX_HEADS, IDX_DIM), iw_s.reshape(DB, TP * IDX_HEADS, 1),
                             pad_t(ik), cache_idx_k[l], n_pages, ksel)
        dq_s = jnp.pad(dq.reshape(DB, TS, DSA_KV_HEADS, DSA_GROUP, LANES).transpose(0, 2, 3, 1, 4),
                       ((0, 0), (0, 0), (0, 0), (0, TP - TS), (0, 0))).reshape(DB, DSA_KV_HEADS, DSA_GROUP * TP, LANES)
        olat, odsa = _sample_attend(
            pt, qlat.reshape(DB, TS * MLA_HEADS, MLA_D_CKV), qr.reshape(DB, TS * MLA_HEADS, MLA_D_ROPE), dq_s, bias,
            (pad_t(ckv), pad_t(kr), pad_t(dk), pad_t(dv)),
            (cache_mla_ckv[l], cache_mla_krope[l], cache_dsa_k[l].reshape(-1, PAGE_SIZE, 2 * LANES),
             cache_dsa_v[l].reshape(-1, PAGE_SIZE, 2 * LANES)), n_pages)
        olat_h = b16(olat.reshape(DB * TS, MLA_HEADS, MLA_D_CKV).transpose(1, 0, 2))
        o_mla = _mla_up_v(olat_h, b16(wuv.transpose(0, 2, 1)))
        o_dsa = b16(odsa.reshape(DB, DSA_KV_HEADS, DSA_GROUP, TP, LANES)[:, :, :, :TS]
                    .transpose(0, 3, 1, 2, 4).reshape(DB * TS, DSA_HEADS * LANES))
        mq_s = jnp.pad(mq.reshape(DB, TS, -1), ((0, 0), (0, TP - TS), (0, 0)))
        Ms = cache_mem_k.shape[2]
        o_mem = _mem_attend(mq_s, cache_mem_k[l].reshape(DB, Ms, -1), cache_mem_v[l].reshape(DB, Ms, -1))
        o_mem = o_mem[:, :TS].reshape(DB * TS, -1)
        xs = _merge(xs, o_mla, o_dsa, o_mem, c, *merge_w)
        xs = _ffn_half(xs, *ffn2)
        st["ckv_s"].append(ckv.reshape(DB, TS, MLA_D_CKV))
        st["kr_s"].append(kr.reshape(DB, TS, MLA_D_ROPE))
        st["dk_s"].append(dk.reshape(DB, TS, DSA_KV_HEADS, DSA_HEAD_DIM))
        st["dv_s"].append(dv.reshape(DB, TS, DSA_KV_HEADS, DSA_HEAD_DIM))
        st["ik_s"].append(ik.reshape(DB, TS, IDX_DIM))

    return (xp.reshape(B, S, D), xs.reshape(DB, TS, D),
            jnp.stack(st["ckv_p"]), jnp.stack(st["kr_p"]), jnp.stack(st["dk_p"]), jnp.stack(st["dv_p"]),
            jnp.stack(st["ik_p"]), jnp.stack(st["mk_p"]), jnp.stack(st["mv_p"]),
            jnp.stack(st["ckv_s"]), jnp.stack(st["kr_s"]), jnp.stack(st["dk_s"]), jnp.stack(st["dv_s"]),
            jnp.stack(st["ik_s"]))
```

```python
import functools

import jax
import jax.numpy as jnp
from jax import lax
from jax.experimental import pallas as pl
from jax.experimental.pallas import tpu as pltpu

F32, BF16, I32 = jnp.float32, jnp.bfloat16, jnp.int32

D_MODEL = 2048
D_FF = 5632
PAST_LEN = 8192
PAGE_SIZE = 128
ROPE_THETA = 500000.0
EPS = 1e-6
MLA_HEADS, MLA_D_NOPE, MLA_D_ROPE, MLA_D_V, MLA_D_CQ, MLA_D_CKV = 8, 128, 64, 128, 768, 512
MLA_D_QK = MLA_D_NOPE + MLA_D_ROPE
DSA_HEADS, DSA_KV_HEADS, DSA_HEAD_DIM = 8, 2, 128
DSA_GROUP = DSA_HEADS // DSA_KV_HEADS
DSA_ROT = DSA_HEAD_DIM // 4
IDX_HEADS, IDX_DIM = 16, 64
IDX_ROT = IDX_DIM // 4
TOPK_MAX = 256
MEM_HEADS, MEM_HEAD_DIM = 4, 256
N_BRANCH = 3

LANES = 128
SUBLANES = 8
NEG = -1e30
INT_MIN = -2 ** 31
VMEM_LIMIT = 56 * 1024 * 1024

MLA_SCALE = MLA_D_QK ** -0.5
DSA_SCALE = DSA_HEAD_DIM ** -0.5
MEM_SCALE = MEM_HEAD_DIM ** -0.5

C_CQ = 0
C_CKV = C_CQ + MLA_D_CQ
C_DQ = C_CKV + MLA_D_CKV
C_DK = C_DQ + DSA_HEADS * DSA_HEAD_DIM
C_DV = C_DK + DSA_KV_HEADS * DSA_HEAD_DIM
C_IQ = C_DV + DSA_KV_HEADS * DSA_HEAD_DIM
C_MQ = C_IQ + IDX_HEADS * IDX_DIM
C_KRI = C_MQ + MEM_HEADS * MEM_HEAD_DIM
C_IW = C_KRI + LANES
C_GATE = C_IW + LANES
RT_MLA, RT_DSA, RT_IDX, RT_KRI, RT_SLOTS = 0, 3, 6, 9, 14

NT_DIMS = (((1,), (1,)), ((), ()))
FLASH_HEAD_GROUP = 4
SAMPLE_PAGES_PER_STEP = 16


def _cparams(*sem):
    return pltpu.CompilerParams(dimension_semantics=sem, vmem_limit_bytes=VMEM_LIMIT)


def _rms(x, g):
    return x * lax.rsqrt(jnp.mean(x * x, axis=-1, keepdims=True) + EPS) * g


def _dot(a, b):
    return jnp.dot(a, b, preferred_element_type=F32)


def _dot_nt(a, b):
    return lax.dot_general(a, b, NT_DIMS, preferred_element_type=F32)


def _roll(x, s):
    return pltpu.roll(x, s % x.shape[-1], axis=x.ndim - 1)


def _lane_tile(x, width):
    n = width // LANES
    return x if n == 1 else jnp.concatenate([x] * n, axis=1)


def _softmax_init(m_ref, l_ref, acc_ref):
    m_ref[...] = jnp.full(m_ref.shape, NEG, F32)
    l_ref[...] = jnp.zeros(l_ref.shape, F32)
    acc_ref[...] = jnp.zeros(acc_ref.shape, F32)


def _softmax_weights(s, m_ref, l_ref):
    m_prev = m_ref[...]
    m_new = jnp.maximum(m_prev, jnp.max(s, axis=1, keepdims=True))
    alpha = jnp.exp(m_prev - m_new)
    p = jnp.exp(s - _lane_tile(m_new, s.shape[1]))
    l_ref[...] = alpha * l_ref[...] + jnp.sum(p, axis=1, keepdims=True)
    m_ref[...] = m_new
    return p.astype(BF16), alpha


def _softmax_accumulate(p, alpha, v, acc_ref):
    acc_ref[...] = _lane_tile(alpha, acc_ref.shape[-1]) * acc_ref[...] + _dot(p, v)


def _softmax_updates(scores, values, states):
    weights = [_softmax_weights(s, m, l) for s, (m, l, _) in zip(scores, states)]
    for (p, alpha), v, (_, _, acc) in zip(weights, values, states):
        _softmax_accumulate(p, alpha, v, acc)


def _ordinal_to_float(u):
    k = u ^ INT_MIN
    return pltpu.bitcast(k ^ ((k >> 31) & 0x7FFFFFFF), F32)


def _ffn_kernel(x_ref, g_ref, wg_ref, wu_ref, wd_ref, o_ref, h_ref):
    @pl.when(pl.program_id(1) == 0)
    def _():
        x = x_ref[...]
        h_ref[...] = _rms(x, g_ref[...]).astype(BF16)
        o_ref[...] = x

    h = h_ref[...]
    a = _dot(h, wg_ref[...])
    b = _dot(h, wu_ref[...])
    act = (a * jax.nn.sigmoid(a) * b).astype(BF16)
    o_ref[...] += 0.5 * _dot(act, wd_ref[...])


def _ffn_half(x, g, wg, wu, wd):
    T, D = x.shape
    F = wg.shape[1]
    tm = min(T, 512)
    tf = min(F, 512)
    return pl.pallas_call(
        _ffn_kernel,
        grid=(T // tm, F // tf),
        in_specs=[
            pl.BlockSpec((tm, D), lambda i, j: (i, 0)),
            pl.BlockSpec((1, D), lambda i, j: (0, 0)),
            pl.BlockSpec((D, tf), lambda i, j: (0, j)),
            pl.BlockSpec((D, tf), lambda i, j: (0, j)),
            pl.BlockSpec((tf, D), lambda i, j: (j, 0)),
        ],
        out_specs=pl.BlockSpec((tm, D), lambda i, j: (i, 0)),
        out_shape=jax.ShapeDtypeStruct((T, D), F32),
        scratch_shapes=[pltpu.VMEM((tm, D), BF16)],
        compiler_params=_cparams("parallel", "arbitrary"),
    )(x, g, wg, wu, wd)


def _proj_kernel(x_ref, g_ref, w_ref, o_ref, h_ref):
    @pl.when(pl.program_id(1) == 0)
    def _():
        h_ref[...] = _rms(x_ref[...], g_ref[...]).astype(BF16)

    o_ref[...] = _dot(h_ref[...], w_ref[...])


def _project(x, g, w):
    T, D = x.shape
    N = w.shape[1]
    tm = min(T, 1024)
    tn = 512
    return pl.pallas_call(
        _proj_kernel,
        grid=(T // tm, N // tn),
        in_specs=[
            pl.BlockSpec((tm, D), lambda i, j: (i, 0)),
            pl.BlockSpec((1, D), lambda i, j: (0, 0)),
            pl.BlockSpec((D, tn), lambda i, j: (0, j)),
        ],
        out_specs=pl.BlockSpec((tm, tn), lambda i, j: (i, j)),
        out_shape=jax.ShapeDtypeStruct((T, N), F32),
        scratch_shapes=[pltpu.VMEM((tm, D), BF16)],
        compiler_params=_cparams("parallel", "arbitrary"),
    )(x, g, w)


def _rope3(x, rt_ref, slot, half):
    c = rt_ref[:, slot * LANES:(slot + 1) * LANES]
    s1 = rt_ref[:, (slot + 1) * LANES:(slot + 2) * LANES]
    s2 = rt_ref[:, (slot + 2) * LANES:(slot + 3) * LANES]
    return x * c + _roll(x, -half) * s1 + _roll(x, half) * s2


def _epilogue_kernel(c_ref, rt_ref, gcq_ref, gqn_ref, gqr_ref, gckv_ref, gkr_ref, gdq_ref, gdk_ref, gmq_ref,
                     wuq_ref, wx_ref, *outs, sample):
    if sample:
        ckv_o, kr_o, dk_o, dv_o, ik_o, iw_o, qlat_o, qr_o, dq_o, iq_o, mq_o = outs
    else:
        (ckv_o, kr_o, dk_o, dv_o, ik_o, iw_o, qcat_o, kcat_o, vb_o, dq_o, dkb_o, dvb_o, iq_o, ikb_o, mq_o) = outs
    tm = c_ref.shape[0]
    lo = lax.broadcasted_iota(I32, (tm, LANES), 1) < MLA_D_ROPE

    cqn = _rms(c_ref[:, C_CQ:C_CQ + MLA_D_CQ], gcq_ref[...]).astype(BF16)
    q = _dot(cqn, wuq_ref[...])
    rope0 = MLA_HEADS * MLA_D_NOPE
    for p in range(MLA_HEADS // 2):
        xr = q[:, rope0 + p * LANES: rope0 + (p + 1) * LANES]
        x2 = xr * xr
        ss_rope = (jnp.sum(jnp.where(lo, x2, 0.0), axis=1, keepdims=True),
                   jnp.sum(jnp.where(lo, 0.0, x2), axis=1, keepdims=True))
        inv = []
        for k in range(2):
            h = 2 * p + k
            qn = q[:, h * MLA_D_NOPE:(h + 1) * MLA_D_NOPE]
            ss = jnp.sum(qn * qn, axis=1, keepdims=True) + ss_rope[k]
            inv.append(lax.rsqrt(ss * (1.0 / MLA_D_QK) + EPS))
            qn = (qn * inv[k] * gqn_ref[...] * MLA_SCALE).astype(BF16)
            if sample:
                qlat_o[:, h * MLA_D_CKV:(h + 1) * MLA_D_CKV] = _dot(qn, wx_ref[h]).astype(BF16)
            else:
                qcat_o[:, h * 2 * LANES: h * 2 * LANES + LANES] = qn
        xr = xr * jnp.where(lo, inv[0], inv[1]) * gqr_ref[...] * MLA_SCALE
        xr = _rope3(xr, rt_ref, RT_MLA, MLA_D_ROPE // 2)
        if sample:
            qr_o[:, p * LANES:(p + 1) * LANES] = xr.astype(BF16)
        else:
            qcat_o[:, (4 * p + 1) * LANES:(4 * p + 2) * LANES] = jnp.where(lo, xr, 0.0).astype(BF16)
            qcat_o[:, (4 * p + 3) * LANES:(4 * p + 4) * LANES] = jnp.where(lo, _roll(xr, MLA_D_ROPE), 0.0).astype(BF16)

    ckv = _rms(c_ref[:, C_CKV:C_CKV + MLA_D_CKV], gckv_ref[...])
    ckv_o[...] = ckv
    x = c_ref[:, C_KRI:C_KRI + LANES]
    ssk = jnp.sum(jnp.where(lo, x * x, 0.0), axis=1, keepdims=True)
    x = jnp.where(lo, x * lax.rsqrt(ssk * (1.0 / MLA_D_ROPE) + EPS) * gkr_ref[...], x)
    s = RT_KRI
    kri = (x * rt_ref[:, s * LANES:(s + 1) * LANES]
           + _roll(x, -(MLA_D_ROPE // 2)) * rt_ref[:, (s + 1) * LANES:(s + 2) * LANES]
           + _roll(x, MLA_D_ROPE // 2) * rt_ref[:, (s + 2) * LANES:(s + 3) * LANES]
           + _roll(x, -(IDX_ROT // 2)) * rt_ref[:, (s + 3) * LANES:(s + 4) * LANES]
           + _roll(x, IDX_ROT // 2) * rt_ref[:, (s + 4) * LANES:(s + 5) * LANES])
    kr_o[...] = kri[:, :MLA_D_ROPE]
    ik_o[...] = kri[:, MLA_D_ROPE:]
    if not sample:
        ikb_o[...] = kri[:, MLA_D_ROPE:].astype(BF16)
        kv = _dot(ckv.astype(BF16), wx_ref[...])
        vb_o[...] = kv[:, MLA_HEADS * MLA_D_NOPE:].astype(BF16)
        krb = jnp.where(lo, kri, 0.0).astype(BF16)
        for h in range(MLA_HEADS):
            kcat_o[:, h * 2 * LANES: h * 2 * LANES + LANES] = kv[:, h * MLA_D_NOPE:(h + 1) * MLA_D_NOPE].astype(BF16)
            kcat_o[:, h * 2 * LANES + LANES:(h + 1) * 2 * LANES] = krb

    for h in range(DSA_HEADS):
        x = _rms(c_ref[:, C_DQ + h * LANES: C_DQ + (h + 1) * LANES], gdq_ref[...])
        dq_o[:, h * LANES:(h + 1) * LANES] = (_rope3(x, rt_ref, RT_DSA, DSA_ROT // 2) * DSA_SCALE).astype(BF16)
    for g in range(DSA_KV_HEADS):
        x = _rms(c_ref[:, C_DK + g * LANES: C_DK + (g + 1) * LANES], gdk_ref[...])
        x = _rope3(x, rt_ref, RT_DSA, DSA_ROT // 2)
        dk_o[:, g * LANES:(g + 1) * LANES] = x
        if not sample:
            dkb_o[:, g * LANES:(g + 1) * LANES] = x.astype(BF16)
    x = c_ref[:, C_DV:C_DV + DSA_KV_HEADS * DSA_HEAD_DIM]
    dv_o[...] = x
    if not sample:
        dvb_o[...] = x.astype(BF16)

    for p in range(IDX_HEADS // 2):
        x = c_ref[:, C_IQ + p * LANES: C_IQ + (p + 1) * LANES]
        iq_o[:, p * LANES:(p + 1) * LANES] = _rope3(x, rt_ref, RT_IDX, IDX_ROT // 2).astype(BF16)
    iw_o[...] = c_ref[:, C_IW:C_IW + LANES]

    for h in range(MEM_HEADS):
        x = _rms(c_ref[:, C_MQ + h * MEM_HEAD_DIM: C_MQ + (h + 1) * MEM_HEAD_DIM], gmq_ref[...])
        mq_o[:, h * MEM_HEAD_DIM:(h + 1) * MEM_HEAD_DIM] = (x * MEM_SCALE).astype(BF16)


def _epilogue(c, rt, gains, wuq, wx, sample):
    T = c.shape[0]
    tm = min(T, 256)
    n_rt = rt.shape[0] // tm
    row = lambda n: pl.BlockSpec((tm, n), lambda i: (i, 0))
    full = lambda a: pl.BlockSpec(a.shape, lambda i: (0,) * a.ndim)
    widths_f32 = [MLA_D_CKV, MLA_D_ROPE, 2 * LANES, 2 * LANES, IDX_DIM, LANES]
    if sample:
        widths_b16 = [MLA_HEADS * MLA_D_CKV, MLA_HEADS * MLA_D_ROPE, 8 * LANES, 8 * LANES, 8 * LANES]
    else:
        widths_b16 = [16 * LANES, 16 * LANES, 8 * LANES, 8 * LANES, 2 * LANES, 2 * LANES, 8 * LANES, IDX_DIM, 8 * LANES]
    out_shape = ([jax.ShapeDtypeStruct((T, n), F32) for n in widths_f32]
                 + [jax.ShapeDtypeStruct((T, n), BF16) for n in widths_b16])
    return pl.pallas_call(
        functools.partial(_epilogue_kernel, sample=sample),
        grid=(T // tm,),
        in_specs=[pl.BlockSpec((tm, C_GATE), lambda i: (i, 0)),
                  pl.BlockSpec((tm, RT_SLOTS * LANES), lambda i: (i % n_rt, 0))]
                 + [full(g) for g in gains] + [full(wuq), full(wx)],
        out_specs=[row(n) for n in widths_f32 + widths_b16],
        out_shape=out_shape,
        compiler_params=_cparams("parallel"),
    )(c, rt, *gains, wuq, wx)


def _mla_flash_kernel(q_ref, k_ref, v_ref, o_ref, m_ref, l_ref, acc_ref):
    i, j = pl.program_id(1), pl.program_id(2)
    tq, tk = q_ref.shape[0], k_ref.shape[0]
    qk, dv = 2 * LANES, MLA_D_V

    @pl.when(j == 0)
    def _():
        _softmax_init(m_ref, l_ref, acc_ref)

    def step(mask):
        for h0 in range(0, MLA_HEADS, FLASH_HEAD_GROUP):
            heads = range(h0, h0 + FLASH_HEAD_GROUP)
            scores = [_dot_nt(q_ref[:, h * qk:(h + 1) * qk], k_ref[:, h * qk:(h + 1) * qk]) for h in heads]
            if mask is not None:
                scores = [jnp.where(mask, s, NEG) for s in scores]
            _softmax_updates(scores, [v_ref[:, h * dv:(h + 1) * dv] for h in heads],
                             [(m_ref.at[h], l_ref.at[h], acc_ref.at[h]) for h in heads])

    @pl.when(j < i)
    def _():
        step(None)

    @pl.when(j == i)
    def _():
        step(lax.broadcasted_iota(I32, (tq, tk), 1) <= lax.broadcasted_iota(I32, (tq, tk), 0))
        for h in range(MLA_HEADS):
            o_ref[:, h * dv:(h + 1) * dv] = (acc_ref[h] / l_ref[h]).astype(o_ref.dtype)


def _mla_prompt(qcat, kcat, vb, B, T):
    tq = min(T, 512)
    nq = T // tq
    H = MLA_HEADS
    return pl.pallas_call(
        _mla_flash_kernel,
        grid=(B, nq, nq),
        in_specs=[
            pl.BlockSpec((tq, H * 2 * LANES), lambda b, i, j: (b * nq + i, 0)),
            pl.BlockSpec((tq, H * 2 * LANES), lambda b, i, j: (b * nq + jnp.minimum(i, j), 0)),
            pl.BlockSpec((tq, H * MLA_D_V), lambda b, i, j: (b * nq + jnp.minimum(i, j), 0)),
        ],
        out_specs=pl.BlockSpec((tq, H * MLA_D_V), lambda b, i, j: (b * nq + i, 0)),
        out_shape=jax.ShapeDtypeStruct((B * T, H * MLA_D_V), BF16),
        scratch_shapes=[pltpu.VMEM((H, tq, LANES), F32), pltpu.VMEM((H, tq, LANES), F32),
                        pltpu.VMEM((H, tq, MLA_D_V), F32)],
        compiler_params=_cparams("parallel", "parallel", "arbitrary"),
    )(qcat, kcat, vb)


def _select_threshold(count_fn, shape, ksel, n_index_bits):
    def value_step(s, prefix):
        cand = prefix | lax.shift_left(jnp.int32(1), 31 - s)
        cand_f = _ordinal_to_float(cand)
        n = count_fn(lambda x, col: x >= cand_f)
        return jnp.where(n >= ksel, cand, prefix)

    thr = _ordinal_to_float(lax.fori_loop(0, 32, value_step, jnp.zeros(shape, I32)))
    all_visible = count_fn(lambda x, col: x > -jnp.inf) <= ksel
    need = ksel - count_fn(lambda x, col: x > thr)
    n_ge = count_fn(lambda x, col: x >= thr)

    def search_cut():
        def cut_step(s, cut):
            cand = cut | lax.shift_left(jnp.int32(1), n_index_bits - 1 - s)
            n = count_fn(lambda x, col: (x == thr) & (col < cand))
            return jnp.where(n <= need, cand, cut)
        return lax.fori_loop(0, n_index_bits, cut_step, jnp.zeros(shape, I32))

    cut = lax.cond(jnp.max(n_ge) > ksel, search_cut, lambda: jnp.full(shape, 2 ** n_index_bits, I32))
    return all_visible, thr, cut


def _dsa_prompt_kernel(iq_ref, iw_ref, ik_ref, dq_ref, dk_ref, dv_ref, o_ref,
                       key_ref, bias_ref, m_ref, l_ref, acc_ref, *, ck, ksel):
    i = pl.program_id(1)
    tq = dq_ref.shape[0]
    T = ik_ref.shape[0]
    nch = ((i + 1) * tq + ck - 1) // ck
    row = i * tq + lax.broadcasted_iota(I32, (tq, ck), 0)
    col0 = lax.broadcasted_iota(I32, (tq, ck), 1)

    def idx_chunk(c, carry):
        ikc = ik_ref[pl.ds(pl.multiple_of(c * ck, ck), ck), :]
        acc = jnp.zeros((tq, ck), F32)
        for h in range(IDX_HEADS):
            acc = acc + jnp.maximum(_dot_nt(iq_ref[h], ikc), 0.0) * iw_ref[:, h:h + 1]
        key_ref[c] = jnp.where(c * ck + col0 <= row, acc, -jnp.inf)
        return carry

    lax.fori_loop(0, nch, idx_chunk, 0)

    def count_fn(pred):
        def body(c, a):
            hit = jnp.where(pred(key_ref[c], c * ck + col0), 1.0, 0.0)
            for t in range(ck // LANES):
                a = a + hit[:, t * LANES:(t + 1) * LANES]
            return a
        a = lax.fori_loop(0, nch, body, jnp.zeros((tq, LANES), F32))
        return jnp.sum(a, axis=1, keepdims=True)

    all_visible, thr, cut = _select_threshold(count_fn, (tq, 1), ksel, T.bit_length())

    def bias_chunk(c, carry):
        x = key_ref[c]
        col = c * ck + col0
        sel = (col <= row) & (all_visible | (x > thr) | ((x == thr) & (col < cut)))
        bias_ref[c] = jnp.where(sel, 0.0, NEG)
        return carry

    lax.fori_loop(0, nch, bias_chunk, 0)

    qg = [jnp.concatenate([dq_ref[:, (g * DSA_GROUP + r) * LANES:(g * DSA_GROUP + r + 1) * LANES]
                           for r in range(DSA_GROUP)], axis=0) for g in range(DSA_KV_HEADS)]
    _softmax_init(m_ref, l_ref, acc_ref)

    def attn_chunk(c, carry):
        k0 = pl.multiple_of(c * ck, ck)
        bias = jnp.concatenate([bias_ref[c]] * DSA_GROUP, axis=0)
        heads = range(DSA_KV_HEADS)
        scores = [_dot_nt(qg[g], dk_ref[pl.ds(k0, ck), g * LANES:(g + 1) * LANES]) + bias for g in heads]
        _softmax_updates(scores, [dv_ref[pl.ds(k0, ck), g * LANES:(g + 1) * LANES] for g in heads],
                         [(m_ref.at[g], l_ref.at[g], acc_ref.at[g]) for g in heads])
        return carry

    lax.fori_loop(0, nch, attn_chunk, 0)
    for g in range(DSA_KV_HEADS):
        o = acc_ref[g] / l_ref[g]
        for r in range(DSA_GROUP):
            o_ref[:, (g * DSA_GROUP + r) * LANES:(g * DSA_GROUP + r + 1) * LANES] = (
                o[r * tq:(r + 1) * tq].astype(o_ref.dtype))


def _dsa_prompt(iq3, iw, ikb, dq, dkb, dvb, B, T):
    tq = min(T, 256)
    ck = min(T, 512)
    nq = T // tq
    ksel = min(TOPK_MAX, T // 4)
    return pl.pallas_call(
        functools.partial(_dsa_prompt_kernel, ck=ck, ksel=ksel),
        grid=(B, nq),
        in_specs=[
            pl.BlockSpec((IDX_HEADS, tq, IDX_DIM), lambda b, i: (0, b * nq + i, 0)),
            pl.BlockSpec((tq, LANES), lambda b, i: (b * nq + i, 0)),
            pl.BlockSpec((T, IDX_DIM), lambda b, i: (b, 0)),
            pl.BlockSpec((tq, DSA_HEADS * LANES), lambda b, i: (b * nq + i, 0)),
            pl.BlockSpec((T, DSA_KV_HEADS * LANES), lambda b, i: (b, 0)),
            pl.BlockSpec((T, DSA_KV_HEADS * LANES), lambda b, i: (b, 0)),
        ],
        out_specs=pl.BlockSpec((tq, DSA_HEADS * LANES), lambda b, i: (b * nq + i, 0)),
        out_shape=jax.ShapeDtypeStruct((B * T, DSA_HEADS * LANES), BF16),
        scratch_shapes=[pltpu.VMEM((T // ck, tq, ck), F32), pltpu.VMEM((T // ck, tq, ck), F32)]
                       + [pltpu.VMEM((DSA_KV_HEADS, DSA_GROUP * tq, LANES), F32)] * 3,
        compiler_params=_cparams("parallel", "arbitrary"),
    )(iq3, iw, ikb, dq, dkb, dvb)


def _memkv_kernel(x_ref, g_ref, w_ref, gk_ref, k_ref, v_ref):
    h = _rms(x_ref[...], g_ref[...]).astype(BF16)
    kv = _dot(h, w_ref[...])
    n = MEM_HEADS * MEM_HEAD_DIM
    for hd in range(MEM_HEADS):
        k_ref[:, hd * MEM_HEAD_DIM:(hd + 1) * MEM_HEAD_DIM] = _rms(
            kv[:, hd * MEM_HEAD_DIM:(hd + 1) * MEM_HEAD_DIM], gk_ref[...])
    v_ref[...] = kv[:, n:]


def _memory_kv(mem, g_in, w, g_k):
    T, D = mem.shape
    n = MEM_HEADS * MEM_HEAD_DIM
    tm = min(T, 256)
    return pl.pallas_call(
        _memkv_kernel,
        grid=(T // tm,),
        in_specs=[pl.BlockSpec((tm, D), lambda i: (i, 0)), pl.BlockSpec((1, D), lambda i: (0, 0)),
                  pl.BlockSpec((D, 2 * n), lambda i: (0, 0)), pl.BlockSpec((1, MEM_HEAD_DIM), lambda i: (0, 0))],
        out_specs=[pl.BlockSpec((tm, n), lambda i: (i, 0))] * 2,
        out_shape=[jax.ShapeDtypeStruct((T, n), F32)] * 2,
        compiler_params=_cparams("parallel"),
    )(mem, g_in, w, g_k)


def _mem_attend_kernel(q_ref, k_ref, v_ref, o_ref):
    for h in range(MEM_HEADS):
        sl = slice(h * MEM_HEAD_DIM, (h + 1) * MEM_HEAD_DIM)
        s = _dot_nt(q_ref[:, sl], k_ref[:, h, :].astype(BF16))
        p = jnp.exp(s - jnp.max(s, axis=1, keepdims=True))
        p = p / jnp.sum(p, axis=1, keepdims=True)
        o_ref[:, sl] = _dot(p.astype(BF16), v_ref[:, h, :].astype(BF16)).astype(o_ref.dtype)


def _mem_attend(mq, mk, mv):
    G, R, n = mq.shape
    M = mk.shape[1]
    tm = min(R, 512)
    mem = pl.BlockSpec((None, M, MEM_HEADS, MEM_HEAD_DIM), lambda g, i: (g, 0, 0, 0))
    return pl.pallas_call(
        _mem_attend_kernel,
        grid=(G, R // tm),
        in_specs=[pl.BlockSpec((None, tm, n), lambda g, i: (g, i, 0)), mem, mem],
        out_specs=pl.BlockSpec((None, tm, n), lambda g, i: (g, i, 0)),
        out_shape=jax.ShapeDtypeStruct((G, R, n), BF16),
        compiler_params=_cparams("parallel", "parallel"),
    )(mq, mk, mv)


def _merge_kernel(x_ref, oa_ref, ob_ref, oc_ref, ga_ref, gb_ref, gc_ref, wa_ref, wb_ref, wc_ref, wo_ref, o_ref):
    @pl.when(pl.program_id(1) == 0)
    def _():
        o_ref[...] = x_ref[...]

    merged = (jax.nn.sigmoid(ga_ref[...]) * _dot(oa_ref[...], wa_ref[...])
              + jax.nn.sigmoid(gb_ref[...]) * _dot(ob_ref[...], wb_ref[...])
              + jax.nn.sigmoid(gc_ref[...]) * _dot(oc_ref[...], wc_ref[...]))
    o_ref[...] += _dot(merged.astype(BF16), wo_ref[...])


def _merge(x, oa, ob, oc, c, wa, wb, wc, wo):
    T, D = x.shape
    n = oa.shape[1]
    tm = min(T, 512)
    tn = min(D, 512)
    g0 = C_GATE // tn
    nb = D // tn
    gate = lambda k: pl.BlockSpec((tm, tn), lambda i, j: (i, g0 + k * nb + j))
    branch_w = pl.BlockSpec((n, tn), lambda i, j: (0, j))
    rows = pl.BlockSpec((tm, n), lambda i, j: (i, 0))
    return pl.pallas_call(
        _merge_kernel,
        grid=(T // tm, nb),
        in_specs=[pl.BlockSpec((tm, D), lambda i, j: (i, 0)), rows, rows, rows, gate(0), gate(1), gate(2),
                  branch_w, branch_w, branch_w, pl.BlockSpec((tn, D), lambda i, j: (j, 0))],
        out_specs=pl.BlockSpec((tm, D), lambda i, j: (i, 0)),
        out_shape=jax.ShapeDtypeStruct((T, D), F32),
        compiler_params=_cparams("parallel", "arbitrary"),
    )(x, oa, ob, oc, c, c, c, wa, wb, wc, wo)


def _sample_scores_kernel(pt_ref, iq_ref, iw_ref, ikn_ref, *rest, n_pages, group):
    page_refs, o_ref = rest[:n_pages], rest[n_pages]
    R = iq_ref.shape[0]
    TP = R // IDX_HEADS
    iq = iq_ref[...]
    iw = iw_ref[...]

    def head_sum(s):
        s = jnp.maximum(s, 0.0) * iw
        return jnp.sum(s.reshape(TP, IDX_HEADS, s.shape[1]), axis=1)

    for g0 in range(0, n_pages, group):
        kt = jnp.concatenate([page_refs[k][...].astype(BF16) for k in range(g0, g0 + group)], axis=1)
        o_ref[:, g0 * PAGE_SIZE:(g0 + group) * PAGE_SIZE] = head_sum(_dot(iq, kt))
    o_ref[:, n_pages * PAGE_SIZE:] = head_sum(_dot_nt(iq, ikn_ref[...].astype(BF16)))


def _sample_scores(pt, iq, iw, ik_new, cache_ikt, n_pages):
    DB, R, _ = iq.shape
    TP = R // IDX_HEADS
    W = (n_pages + 1) * PAGE_SIZE

    def page_spec(k):
        return pl.BlockSpec((None, IDX_DIM, PAGE_SIZE), lambda b, pt: (pt[b * n_pages + k], 0, 0))

    grid_spec = pltpu.PrefetchScalarGridSpec(
        num_scalar_prefetch=1,
        grid=(DB,),
        in_specs=[pl.BlockSpec((None, R, IDX_DIM), lambda b, pt: (b, 0, 0)),
                  pl.BlockSpec((None, R, 1), lambda b, pt: (b, 0, 0)),
                  pl.BlockSpec((None, PAGE_SIZE, IDX_DIM), lambda b, pt: (b, 0, 0))]
                 + [page_spec(k) for k in range(n_pages)],
        out_specs=pl.BlockSpec((None, TP, W), lambda b, pt: (b, 0, 0)),
    )
    return pl.pallas_call(
        functools.partial(_sample_scores_kernel, n_pages=n_pages, group=min(n_pages, 8)),
        grid_spec=grid_spec,
        out_shape=jax.ShapeDtypeStruct((DB, TP, W), F32),
        compiler_params=_cparams("parallel"),
    )(pt, iq, iw, ik_new, *([cache_ikt] * n_pages))


def _sample_select_kernel(s_ref, o_ref, *, past, ksel, tp):
    RB, W = s_ref.shape
    t = lax.broadcasted_iota(I32, (RB, W), 0) % tp
    col = lax.broadcasted_iota(I32, (RB, W), 1)
    visible = col <= past + t
    x = jnp.where(visible, s_ref[...], -jnp.inf)

    def count_fn(pred):
        hit = jnp.where(pred(x, col), 1.0, 0.0)
        a = hit[:, :LANES]
        for c in range(1, W // LANES):
            a = a + hit[:, c * LANES:(c + 1) * LANES]
        return jnp.sum(a, axis=1, keepdims=True)

    all_visible, thr, cut = _select_threshold(count_fn, (RB, 1), ksel, W.bit_length())
    sel = visible & (all_visible | (x > thr) | ((x == thr) & (col < cut)))
    o_ref[...] = jnp.where(sel, 1.0, 0.0).astype(o_ref.dtype)


def _sample_select(scores, past, ksel, tp):
    R, W = scores.shape
    rb = min(R, 64)
    return pl.pallas_call(
        functools.partial(_sample_select_kernel, past=past, ksel=ksel, tp=tp),
        grid=(R // rb,),
        in_specs=[pl.BlockSpec((rb, W), lambda i: (i, 0))],
        out_specs=pl.BlockSpec((rb, W), lambda i: (i, 0)),
        out_shape=jax.ShapeDtypeStruct((R, W), BF16),
        compiler_params=_cparams("parallel"),
    )(scores)


def _sample_attend_kernel(pt_ref, qlat_ref, qr_ref, dq_ref, sel_ref, seln_ref,
                          ckvn_ref, krn_ref, dkn_ref, dvn_ref, *rest, pages, n_steps):
    ckv_pages, krt_pages = rest[:pages], rest[pages:2 * pages]
    dk_pages, dv_pages = rest[2 * pages:3 * pages], rest[3 * pages:4 * pages]
    (olat_ref, odsa_ref, ma_ref, la_ref, acca_ref, md_ref, ld_ref, accd_ref,
     ckv_cat, krt_cat, dk_cat, dv_cat) = rest[4 * pages:]
    j = pl.program_id(1)
    RA, RD = qlat_ref.shape[0], dq_ref.shape[1]
    heads = range(DSA_KV_HEADS)
    states = [(ma_ref, la_ref, acca_ref)] + [(md_ref.at[g], ld_ref.at[g], accd_ref.at[g]) for g in heads]

    @pl.when(j == 0)
    def _():
        _softmax_init(ma_ref, la_ref, acca_ref)
        _softmax_init(md_ref, ld_ref, accd_ref)

    for k in range(pages):
        ckv_cat[k * PAGE_SIZE:(k + 1) * PAGE_SIZE, :] = ckv_pages[k][...].astype(BF16)
        krt_cat[:, k * PAGE_SIZE:(k + 1) * PAGE_SIZE] = krt_pages[k][...].astype(BF16)
        dk_cat[k * 2 * PAGE_SIZE:(k + 1) * 2 * PAGE_SIZE, :] = dk_pages[k][...].astype(BF16)
        dv_cat[k * 2 * PAGE_SIZE:(k + 1) * 2 * PAGE_SIZE, :] = dv_pages[k][...].astype(BF16)

    qlat, qr = qlat_ref[...], qr_ref[...]
    ckv, dk, dv = ckv_cat[...], dk_cat[...], dv_cat[...]
    selected = jnp.concatenate([sel_ref[...].astype(F32)] * DSA_GROUP, axis=0) > 0.5
    head_of_col = lax.broadcasted_iota(I32, selected.shape, 1) % DSA_KV_HEADS
    scores = [_dot_nt(qlat, ckv) + _dot(qr, krt_cat[...])]
    scores += [jnp.where(selected & (head_of_col == g), _dot_nt(dq_ref[g], dk), NEG) for g in heads]
    _softmax_updates(scores, [ckv, dv, dv], states)

    @pl.when(j == n_steps - 1)
    def _():
        t = lax.broadcasted_iota(I32, (RA, PAGE_SIZE), 0) // MLA_HEADS
        cn = lax.broadcasted_iota(I32, (RA, PAGE_SIZE), 1)
        ckvn = ckvn_ref[...].astype(BF16)
        s = _dot_nt(qlat, ckvn) + _dot_nt(qr, krn_ref[...].astype(BF16))
        seln = jnp.concatenate([seln_ref[...].astype(F32)] * DSA_GROUP, axis=0) > 0.5
        dkn, dvn = dkn_ref[...].astype(BF16), dvn_ref[...].astype(BF16)
        scores = [jnp.where(cn <= t, s, NEG)]
        scores += [jnp.where(seln, _dot_nt(dq_ref[g], dkn[:, g * LANES:(g + 1) * LANES]), NEG) for g in heads]
        _softmax_updates(scores, [ckvn] + [dvn[:, g * LANES:(g + 1) * LANES] for g in heads], states)
        olat_ref[...] = acca_ref[...] / _lane_tile(la_ref[...], MLA_D_CKV)
        for g in heads:
            odsa_ref[g] = accd_ref[g] / ld_ref[g]


def _sample_attend(pt, qlat, qr, dq, sel, sel_new, new_rows, caches, n_pages):
    DB, RA, _ = qlat.shape
    RD = dq.shape[2]
    TP = sel.shape[1]
    pages = min(n_pages, SAMPLE_PAGES_PER_STEP)
    n_steps = n_pages // pages
    page_of = lambda b, j, pt, k: pt[b * n_pages + j * pages + k]
    page_specs = (
        [pl.BlockSpec((None, PAGE_SIZE, MLA_D_CKV), lambda b, j, pt, k=k: (page_of(b, j, pt, k), 0, 0))
         for k in range(pages)]
        + [pl.BlockSpec((None, MLA_D_ROPE, PAGE_SIZE), lambda b, j, pt, k=k: (page_of(b, j, pt, k), 0, 0))
           for k in range(pages)]
        + [pl.BlockSpec((DSA_KV_HEADS * PAGE_SIZE, LANES), lambda b, j, pt, k=k: (page_of(b, j, pt, k), 0))
           for k in range(pages)] * 2)
    per_seq = lambda shape: pl.BlockSpec((None,) + shape, lambda b, j, pt: (b,) + (0,) * len(shape))
    in_specs = [per_seq((RA, MLA_D_CKV)), per_seq((RA, MLA_D_ROPE)), per_seq((DSA_KV_HEADS, RD, LANES)),
                pl.BlockSpec((None, TP, pages * 2 * PAGE_SIZE), lambda b, j, pt: (b, 0, j)), per_seq((TP, PAGE_SIZE))]
    in_specs += [per_seq((PAGE_SIZE, w)) for w in (MLA_D_CKV, MLA_D_ROPE, 2 * LANES, 2 * LANES)]
    n_keys = pages * PAGE_SIZE
    grid_spec = pltpu.PrefetchScalarGridSpec(
        num_scalar_prefetch=1,
        grid=(DB, n_steps),
        in_specs=in_specs + page_specs,
        out_specs=[per_seq((RA, MLA_D_CKV)), per_seq((DSA_KV_HEADS, RD, LANES))],
        scratch_shapes=[pltpu.VMEM((RA, LANES), F32), pltpu.VMEM((RA, LANES), F32), pltpu.VMEM((RA, MLA_D_CKV), F32)]
                       + [pltpu.VMEM((DSA_KV_HEADS, RD, LANES), F32)] * 3
                       + [pltpu.VMEM((n_keys, MLA_D_CKV), BF16), pltpu.VMEM((MLA_D_ROPE, n_keys), BF16),
                          pltpu.VMEM((2 * n_keys, LANES), BF16), pltpu.VMEM((2 * n_keys, LANES), BF16)],
    )
    operands = [qlat, qr, dq, sel, sel_new, *new_rows]
    for c in caches:
        operands += [c] * pages
    return pl.pallas_call(
        functools.partial(_sample_attend_kernel, pages=pages, n_steps=n_steps),
        grid_spec=grid_spec,
        out_shape=[jax.ShapeDtypeStruct((DB, RA, MLA_D_CKV), F32),
                   jax.ShapeDtypeStruct((DB, DSA_KV_HEADS, RD, LANES), F32)],
        compiler_params=_cparams("parallel", "arbitrary"),
    )(pt, *operands)


def _uv_kernel(o_ref, w_ref, out_ref):
    out_ref[...] = _dot(o_ref[...], w_ref[...]).astype(out_ref.dtype)


def _mla_up_v(olat_h, wuv_t):
    H, T, C = olat_h.shape
    return pl.pallas_call(
        _uv_kernel,
        grid=(H,),
        in_specs=[pl.BlockSpec((None, T, C), lambda h: (h, 0, 0)), pl.BlockSpec((None, C, MLA_D_V), lambda h: (h, 0, 0))],
        out_specs=pl.BlockSpec((T, MLA_D_V), lambda h: (0, h)),
        out_shape=jax.ShapeDtypeStruct((T, H * MLA_D_V), BF16),
        compiler_params=_cparams("parallel"),
    )(olat_h, wuv_t)


def _rope_tables(pos):
    pos = pos.astype(F32)
    T = pos.shape[0]

    def tabs(rot, period):
        half = rot // 2
        inv_freq = ROPE_THETA ** (-jnp.arange(half, dtype=F32) / half)
        ang = pos[:, None] * inv_freq[None, :]
        cos, sin = jnp.cos(ang), jnp.sin(ang)
        one, zero = jnp.ones((T, period - rot), F32), jnp.zeros((T, period - rot), F32)
        zh = jnp.zeros((T, half), F32)
        return (jnp.concatenate([cos, cos, one], 1), jnp.concatenate([-sin, zh, zero], 1),
                jnp.concatenate([zh, sin, zero], 1))

    rep = lambda ts, period: [jnp.tile(t, (1, LANES // period)) for t in ts]
    mla, idx = tabs(MLA_D_ROPE, MLA_D_ROPE), tabs(IDX_ROT, IDX_DIM)
    z = jnp.zeros((T, LANES // 2), F32)
    kri = [jnp.concatenate([mla[0], idx[0]], 1), jnp.concatenate([mla[1], z], 1), jnp.concatenate([mla[2], z], 1),
           jnp.concatenate([z, idx[1]], 1), jnp.concatenate([z, idx[2]], 1)]
    return jnp.concatenate(rep(mla, MLA_D_ROPE) + rep(tabs(DSA_ROT, LANES), LANES) + rep(idx, IDX_DIM) + kri, axis=1)


def _permute_w_in(w_in):
    offs, off = {}, 0
    for name, w in (("mla_cq", MLA_D_CQ), ("mla_ckv", MLA_D_CKV), ("mla_kr", MLA_D_ROPE),
                    ("dsa_q", DSA_HEADS * DSA_HEAD_DIM), ("dsa_k", DSA_KV_HEADS * DSA_HEAD_DIM),
                    ("dsa_v", DSA_KV_HEADS * DSA_HEAD_DIM), ("idx_q", IDX_HEADS * IDX_DIM), ("idx_w", IDX_HEADS),
                    ("idx_k", IDX_DIM), ("mem_q", MEM_HEADS * MEM_HEAD_DIM), ("gates", N_BRANCH * D_MODEL)):
        offs[name] = (off, w)
        off += w
    col = lambda n: w_in[:, offs[n][0]: offs[n][0] + offs[n][1]]
    pad = jnp.zeros((w_in.shape[0], LANES - IDX_HEADS), w_in.dtype)
    return jnp.concatenate([col("mla_cq"), col("mla_ckv"), col("dsa_q"), col("dsa_k"), col("dsa_v"), col("idx_q"),
                            col("mem_q"), col("mla_kr"), col("idx_k"), col("idx_w"), pad, col("gates")], axis=1)


def kernel(x_prompt, x_sample, cache_mla_ckv, cache_mla_krope, cache_dsa_k, cache_dsa_v, cache_idx_k, cache_mem_k,
           cache_mem_v, page_table, mem_prompt, g_ffn1, w_ffn1_gate, w_ffn1_up, w_ffn1_down, g_mix, w_in, g_mla_cq,
           w_mla_uq, g_mla_q, w_mla_uk, w_mla_uv, g_mla_ckv, g_mla_kr, g_dsa_q, g_dsa_k, g_mem_in, w_mem_kv,
           g_mem_q, g_mem_k, w_o_mla, w_o_dsa, w_o_mem, w_out, g_ffn2, w_ffn2_gate, w_ffn2_up, w_ffn2_down):
    B, S, D = x_prompt.shape
    DB, TS, _ = x_sample.shape
    depth = g_mix.shape[0]
    n_pages = page_table.shape[1]
    past = n_pages * PAGE_SIZE
    TP = SUBLANES
    xp = x_prompt.reshape(B * S, D)
    xs = x_sample.reshape(DB * TS, D)
    pt = page_table.reshape(-1)
    rt_p = _rope_tables(jnp.arange(S))
    rt_s = jnp.tile(_rope_tables(past + jnp.arange(TS)), (DB, 1))
    row = lambda g: g.reshape(1, -1).astype(F32)
    b16 = lambda w: w.astype(BF16)
    st = {k: [] for k in ("ckv_p", "kr_p", "dk_p", "dv_p", "ik_p", "mk_p", "mv_p", "ckv_s", "kr_s", "dk_s", "dv_s", "ik_s")}

    for l in range(depth):
        ffn1 = (row(g_ffn1[l]), b16(w_ffn1_gate[l]), b16(w_ffn1_up[l]), b16(w_ffn1_down[l]))
        ffn2 = (row(g_ffn2[l]), b16(w_ffn2_gate[l]), b16(w_ffn2_up[l]), b16(w_ffn2_down[l]))
        w_in_p = b16(_permute_w_in(w_in[l]))
        gq = g_mla_q[l]
        gains = (row(g_mla_cq[l]), row(gq[:MLA_D_NOPE]), row(jnp.tile(gq[MLA_D_NOPE:], 2)), row(g_mla_ckv[l]),
                 row(jnp.concatenate([g_mla_kr[l], jnp.ones((LANES - MLA_D_ROPE,), F32)])),
                 row(g_dsa_q[l]), row(g_dsa_k[l]), row(g_mem_q[l]))
        wuq = w_mla_uq[l].reshape(MLA_D_CQ, MLA_HEADS, MLA_D_QK)
        wuq = b16(jnp.concatenate([wuq[:, :, :MLA_D_NOPE].reshape(MLA_D_CQ, -1),
                                   wuq[:, :, MLA_D_NOPE:].reshape(MLA_D_CQ, -1)], axis=1))
        wuk, wuv = w_mla_uk[l], w_mla_uv[l]
        w_kv = b16(jnp.concatenate([wuk.transpose(2, 0, 1).reshape(MLA_D_CKV, -1),
                                    wuv.transpose(2, 0, 1).reshape(MLA_D_CKV, -1)], axis=1))
        merge_w = (b16(w_o_mla[l]), b16(w_o_dsa[l]), b16(w_o_mem[l]), b16(w_out[l]))

        mk_p, mv_p = _memory_kv(mem_prompt.reshape(-1, D), row(g_mem_in[l]), b16(w_mem_kv[l]), row(g_mem_k[l]))
        M = mem_prompt.shape[1]
        xp = _ffn_half(xp, *ffn1)
        c = _project(xp, row(g_mix[l]), w_in_p)
        (ckv, kr, dk, dv, ik, iw, qcat, kcat, vb, dq, dkb, dvb, iq, ikb, mq) = _epilogue(
            c, rt_p, gains, wuq, w_kv, sample=False)
        o_mla = _mla_prompt(qcat, kcat, vb, B, S)
        iq3 = iq.reshape(B * S, IDX_HEADS, IDX_DIM).transpose(1, 0, 2)
        o_dsa = _dsa_prompt(iq3, iw, ikb, dq, dkb, dvb, B, S)
        mem_shape = (B, M, MEM_HEADS, MEM_HEAD_DIM)
        o_mem = _mem_attend(mq.reshape(B, S, -1), mk_p.reshape(mem_shape), mv_p.reshape(mem_shape)).reshape(B * S, -1)
        xp = _merge(xp, o_mla, o_dsa, o_mem, c, *merge_w)
        xp = _ffn_half(xp, *ffn2)
        st["ckv_p"].append(ckv.reshape(B, S, MLA_D_CKV))
        st["kr_p"].append(kr.reshape(B, S, MLA_D_ROPE))
        st["dk_p"].append(dk.reshape(B, S, DSA_KV_HEADS, DSA_HEAD_DIM))
        st["dv_p"].append(dv.reshape(B, S, DSA_KV_HEADS, DSA_HEAD_DIM))
        st["ik_p"].append(ik.reshape(B, S, IDX_DIM))
        st["mk_p"].append(mk_p.reshape(B, M, MEM_HEADS, MEM_HEAD_DIM))
        st["mv_p"].append(mv_p.reshape(B, M, MEM_HEADS, MEM_HEAD_DIM))

        xs = _ffn_half(xs, *ffn1)
        c = _project(xs, row(g_mix[l]), w_in_p)
        (ckv, kr, dk, dv, ik, iw, qlat, qr, dq, iq, mq) = _epilogue(c, rt_s, gains, wuq, b16(wuk), sample=True)
        pad_t = lambda a: jnp.pad(a.reshape(DB, TS, -1), ((0, 0), (0, PAGE_SIZE - TS), (0, 0)))
        iq_s = jnp.pad(iq.reshape(DB, TS, IDX_HEADS, IDX_DIM), ((0, 0), (0, TP - TS), (0, 0), (0, 0)))
        iw_s = jnp.pad(iw[:, :IDX_HEADS].reshape(DB, TS, IDX_HEADS), ((0, 0), (0, TP - TS), (0, 0)))
        ksel = min(TOPK_MAX, (past + TS) // 4)
        scores = _sample_scores(pt, iq_s.reshape(DB, TP * IDX_HEADS, IDX_DIM), iw_s.reshape(DB, TP * IDX_HEADS, 1),
                                pad_t(ik), jnp.swapaxes(cache_idx_k[l], 1, 2), n_pages)
        sel = _sample_select(scores.reshape(DB * TP, -1), past, ksel, TP).reshape(DB, TP, -1)
        dq_s = jnp.pad(dq.reshape(DB, TS, DSA_KV_HEADS, DSA_GROUP, LANES).transpose(0, 2, 3, 1, 4),
                       ((0, 0), (0, 0), (0, 0), (0, TP - TS), (0, 0))).reshape(DB, DSA_KV_HEADS, DSA_GROUP * TP, LANES)
        olat, odsa = _sample_attend(
            pt, qlat.reshape(DB, TS * MLA_HEADS, MLA_D_CKV), qr.reshape(DB, TS * MLA_HEADS, MLA_D_ROPE), dq_s,
            jnp.repeat(sel[:, :, :past], DSA_KV_HEADS, axis=2), sel[:, :, past:],
            (pad_t(ckv), pad_t(kr), pad_t(dk), pad_t(dv)),
            (cache_mla_ckv[l], jnp.swapaxes(cache_mla_krope[l], 1, 2), cache_dsa_k[l].reshape(-1, LANES),
             cache_dsa_v[l].reshape(-1, LANES)), n_pages)
        olat_h = b16(olat.reshape(DB * TS, MLA_HEADS, MLA_D_CKV).transpose(1, 0, 2))
        o_mla = _mla_up_v(olat_h, b16(wuv.transpose(0, 2, 1)))
        o_dsa = b16(odsa.reshape(DB, DSA_KV_HEADS, DSA_GROUP, TP, LANES)[:, :, :, :TS]
                    .transpose(0, 3, 1, 2, 4).reshape(DB * TS, DSA_HEADS * LANES))
        mq_s = jnp.pad(mq.reshape(DB, TS, -1), ((0, 0), (0, TP - TS), (0, 0)))
        o_mem = _mem_attend(mq_s, cache_mem_k[l], cache_mem_v[l])[:, :TS].reshape(DB * TS, -1)
        xs = _merge(xs, o_mla, o_dsa, o_mem, c, *merge_w)
        xs = _ffn_half(xs, *ffn2)
        st["ckv_s"].append(ckv.reshape(DB, TS, MLA_D_CKV))
        st["kr_s"].append(kr.reshape(DB, TS, MLA_D_ROPE))
        st["dk_s"].append(dk.reshape(DB, TS, DSA_KV_HEADS, DSA_HEAD_DIM))
        st["dv_s"].append(dv.reshape(DB, TS, DSA_KV_HEADS, DSA_HEAD_DIM))
        st["ik_s"].append(ik.reshape(DB, TS, IDX_DIM))

    return (xp.reshape(B, S, D), xs.reshape(DB, TS, D),
            jnp.stack(st["ckv_p"]), jnp.stack(st["kr_p"]), jnp.stack(st["dk_p"]), jnp.stack(st["dv_p"]),
            jnp.stack(st["ik_p"]), jnp.stack(st["mk_p"]), jnp.stack(st["mv_p"]),
            jnp.stack(st["ckv_s"]), jnp.stack(st["kr_s"]), jnp.stack(st["dk_s"]), jnp.stack(st["dv_s"]),
            jnp.stack(st["ik_s"]))
```

```python
import functools

import jax
import jax.numpy as jnp
from jax import lax
from jax.experimental import pallas as pl
from jax.experimental.pallas import tpu as pltpu

F32, BF16, I32 = jnp.float32, jnp.bfloat16, jnp.int32

D_MODEL = 2048
D_FF = 5632
PAST_LEN = 8192
PAGE_SIZE = 128
ROPE_THETA = 500000.0
EPS = 1e-6
MLA_HEADS, MLA_D_NOPE, MLA_D_ROPE, MLA_D_V, MLA_D_CQ, MLA_D_CKV = 8, 128, 64, 128, 768, 512
MLA_D_QK = MLA_D_NOPE + MLA_D_ROPE
DSA_HEADS, DSA_KV_HEADS, DSA_HEAD_DIM = 8, 2, 128
DSA_GROUP = DSA_HEADS // DSA_KV_HEADS
DSA_ROT = DSA_HEAD_DIM // 4
IDX_HEADS, IDX_DIM = 16, 64
IDX_ROT = IDX_DIM // 4
TOPK_MAX = 256
MEM_HEADS, MEM_HEAD_DIM = 4, 256
N_BRANCH = 3

LANES = 128
SUBLANES = 8
NEG = -1e30
INT_MIN = -2 ** 31
VMEM_LIMIT = 56 * 1024 * 1024

MLA_SCALE = MLA_D_QK ** -0.5
DSA_SCALE = DSA_HEAD_DIM ** -0.5
MEM_SCALE = MEM_HEAD_DIM ** -0.5

C_CQ = 0
C_CKV = C_CQ + MLA_D_CQ
C_DQ = C_CKV + MLA_D_CKV
C_DK = C_DQ + DSA_HEADS * DSA_HEAD_DIM
C_DV = C_DK + DSA_KV_HEADS * DSA_HEAD_DIM
C_IQ = C_DV + DSA_KV_HEADS * DSA_HEAD_DIM
C_MQ = C_IQ + IDX_HEADS * IDX_DIM
C_KRI = C_MQ + MEM_HEADS * MEM_HEAD_DIM
C_IW = C_KRI + LANES
C_GATE = C_IW + LANES
RT_MLA, RT_DSA, RT_IDX, RT_KRI, RT_SLOTS = 0, 3, 6, 9, 14

NT_DIMS = (((1,), (1,)), ((), ()))
FLASH_HEAD_GROUP = 4
SAMPLE_PAGES_PER_STEP = 16
SAMPLE_CHUNK_PAGES = 4


def _cparams(*sem):
    return pltpu.CompilerParams(dimension_semantics=sem, vmem_limit_bytes=VMEM_LIMIT)


def _rms(x, g):
    return x * lax.rsqrt(jnp.mean(x * x, axis=-1, keepdims=True) + EPS) * g


def _dot(a, b):
    return jnp.dot(a, b, preferred_element_type=F32)


def _dot_nt(a, b):
    return lax.dot_general(a, b, NT_DIMS, preferred_element_type=F32)


def _roll(x, s):
    return pltpu.roll(x, s % x.shape[-1], axis=x.ndim - 1)


def _lane_tile(x, width):
    n = width // LANES
    return x if n == 1 else jnp.concatenate([x] * n, axis=1)


def _softmax_init(m_ref, l_ref, acc_ref):
    m_ref[...] = jnp.full(m_ref.shape, NEG, F32)
    l_ref[...] = jnp.zeros(l_ref.shape, F32)
    acc_ref[...] = jnp.zeros(acc_ref.shape, F32)


def _softmax_weights(s, m_ref, l_ref):
    m_prev = m_ref[...]
    m_new = jnp.maximum(m_prev, jnp.max(s, axis=1, keepdims=True))
    alpha = jnp.exp(m_prev - m_new)
    p = jnp.exp(s - _lane_tile(m_new, s.shape[1]))
    l_ref[...] = alpha * l_ref[...] + jnp.sum(p, axis=1, keepdims=True)
    m_ref[...] = m_new
    return p.astype(BF16), alpha


def _softmax_accumulate(p, alpha, v, acc_ref):
    acc_ref[...] = _lane_tile(alpha, acc_ref.shape[-1]) * acc_ref[...] + _dot(p, v)


def _softmax_updates(scores, values, states):
    weights = [_softmax_weights(s, m, l) for s, (m, l, _) in zip(scores, states)]
    for (p, alpha), v, (_, _, acc) in zip(weights, values, states):
        _softmax_accumulate(p, alpha, v, acc)


def _ordinal_to_float(u):
    k = u ^ INT_MIN
    return pltpu.bitcast(k ^ ((k >> 31) & 0x7FFFFFFF), F32)


def _ffn_kernel(x_ref, g_ref, wg_ref, wu_ref, wd_ref, o_ref, h_ref):
    @pl.when(pl.program_id(1) == 0)
    def _():
        x = x_ref[...]
        h_ref[...] = _rms(x, g_ref[...]).astype(BF16)
        o_ref[...] = x

    h = h_ref[...]
    a = _dot(h, wg_ref[...])
    b = _dot(h, wu_ref[...])
    act = (a * jax.nn.sigmoid(a) * b).astype(BF16)
    o_ref[...] += 0.5 * _dot(act, wd_ref[...])


def _ffn_half(x, g, wg, wu, wd):
    T, D = x.shape
    F = wg.shape[1]
    tm = min(T, 512)
    tf = min(F, 512)
    return pl.pallas_call(
        _ffn_kernel,
        grid=(T // tm, F // tf),
        in_specs=[
            pl.BlockSpec((tm, D), lambda i, j: (i, 0)),
            pl.BlockSpec((1, D), lambda i, j: (0, 0)),
            pl.BlockSpec((D, tf), lambda i, j: (0, j)),
            pl.BlockSpec((D, tf), lambda i, j: (0, j)),
            pl.BlockSpec((tf, D), lambda i, j: (j, 0)),
        ],
        out_specs=pl.BlockSpec((tm, D), lambda i, j: (i, 0)),
        out_shape=jax.ShapeDtypeStruct((T, D), F32),
        scratch_shapes=[pltpu.VMEM((tm, D), BF16)],
        compiler_params=_cparams("parallel", "arbitrary"),
    )(x, g, wg, wu, wd)


def _proj_kernel(x_ref, g_ref, w_ref, o_ref, h_ref):
    @pl.when(pl.program_id(1) == 0)
    def _():
        h_ref[...] = _rms(x_ref[...], g_ref[...]).astype(BF16)

    o_ref[...] = _dot_nt(h_ref[...], w_ref[...])


def _project(x, g, w_t):
    T, D = x.shape
    N = w_t.shape[0]
    tm = min(T, 1024)
    tn = 512
    return pl.pallas_call(
        _proj_kernel,
        grid=(T // tm, N // tn),
        in_specs=[
            pl.BlockSpec((tm, D), lambda i, j: (i, 0)),
            pl.BlockSpec((1, D), lambda i, j: (0, 0)),
            pl.BlockSpec((tn, D), lambda i, j: (j, 0)),
        ],
        out_specs=pl.BlockSpec((tm, tn), lambda i, j: (i, j)),
        out_shape=jax.ShapeDtypeStruct((T, N), F32),
        scratch_shapes=[pltpu.VMEM((tm, D), BF16)],
        compiler_params=_cparams("parallel", "arbitrary"),
    )(x, g, w_t)


def _rope3(x, rt_ref, slot, half):
    c = rt_ref[:, slot * LANES:(slot + 1) * LANES]
    s1 = rt_ref[:, (slot + 1) * LANES:(slot + 2) * LANES]
    s2 = rt_ref[:, (slot + 2) * LANES:(slot + 3) * LANES]
    return x * c + _roll(x, -half) * s1 + _roll(x, half) * s2


def _epilogue_kernel(c_ref, rt_ref, gcq_ref, gqn_ref, gqr_ref, gckv_ref, gkr_ref, gdq_ref, gdk_ref, gmq_ref,
                     wuq_ref, wx_ref, *outs, sample):
    if sample:
        ckv_o, kr_o, dk_o, dv_o, ik_o, iw_o, qlat_o, qr_o, dq_o, iq_o, mq_o = outs
    else:
        (ckv_o, kr_o, dk_o, dv_o, ik_o, iw_o, qcat_o, kcat_o, vb_o, dq_o, dkb_o, dvb_o, iq_o, ikb_o, mq_o) = outs
    tm = c_ref.shape[0]
    lo = lax.broadcasted_iota(I32, (tm, LANES), 1) < MLA_D_ROPE

    cqn = _rms(c_ref[:, C_CQ:C_CQ + MLA_D_CQ], gcq_ref[...]).astype(BF16)
    q = _dot(cqn, wuq_ref[...])
    rope0 = MLA_HEADS * MLA_D_NOPE
    for p in range(MLA_HEADS // 2):
        xr = q[:, rope0 + p * LANES: rope0 + (p + 1) * LANES]
        x2 = xr * xr
        ss_rope = (jnp.sum(jnp.where(lo, x2, 0.0), axis=1, keepdims=True),
                   jnp.sum(jnp.where(lo, 0.0, x2), axis=1, keepdims=True))
        inv = []
        for k in range(2):
            h = 2 * p + k
            qn = q[:, h * MLA_D_NOPE:(h + 1) * MLA_D_NOPE]
            ss = jnp.sum(qn * qn, axis=1, keepdims=True) + ss_rope[k]
            inv.append(lax.rsqrt(ss * (1.0 / MLA_D_QK) + EPS))
            qn = (qn * inv[k] * gqn_ref[...] * MLA_SCALE).astype(BF16)
            if sample:
                qlat_o[:, h * MLA_D_CKV:(h + 1) * MLA_D_CKV] = _dot(qn, wx_ref[h]).astype(BF16)
            else:
                qcat_o[:, h * 2 * LANES: h * 2 * LANES + LANES] = qn
        xr = xr * jnp.where(lo, inv[0], inv[1]) * gqr_ref[...] * MLA_SCALE
        xr = _rope3(xr, rt_ref, RT_MLA, MLA_D_ROPE // 2)
        if sample:
            qr_o[:, p * LANES:(p + 1) * LANES] = xr.astype(BF16)
        else:
            qcat_o[:, (4 * p + 1) * LANES:(4 * p + 2) * LANES] = jnp.where(lo, xr, 0.0).astype(BF16)
            qcat_o[:, (4 * p + 3) * LANES:(4 * p + 4) * LANES] = jnp.where(lo, _roll(xr, MLA_D_ROPE), 0.0).astype(BF16)

    ckv = _rms(c_ref[:, C_CKV:C_CKV + MLA_D_CKV], gckv_ref[...])
    ckv_o[...] = ckv
    x = c_ref[:, C_KRI:C_KRI + LANES]
    ssk = jnp.sum(jnp.where(lo, x * x, 0.0), axis=1, keepdims=True)
    x = jnp.where(lo, x * lax.rsqrt(ssk * (1.0 / MLA_D_ROPE) + EPS) * gkr_ref[...], x)
    s = RT_KRI
    kri = (x * rt_ref[:, s * LANES:(s + 1) * LANES]
           + _roll(x, -(MLA_D_ROPE // 2)) * rt_ref[:, (s + 1) * LANES:(s + 2) * LANES]
           + _roll(x, MLA_D_ROPE // 2) * rt_ref[:, (s + 2) * LANES:(s + 3) * LANES]
           + _roll(x, -(IDX_ROT // 2)) * rt_ref[:, (s + 3) * LANES:(s + 4) * LANES]
           + _roll(x, IDX_ROT // 2) * rt_ref[:, (s + 4) * LANES:(s + 5) * LANES])
    kr_o[...] = kri[:, :MLA_D_ROPE]
    ik_o[...] = kri[:, MLA_D_ROPE:]
    if not sample:
        ikb_o[...] = kri[:, MLA_D_ROPE:].astype(BF16)
        kv = _dot(ckv.astype(BF16), wx_ref[...])
        vb_o[...] = kv[:, MLA_HEADS * MLA_D_NOPE:].astype(BF16)
        krb = jnp.where(lo, kri, 0.0).astype(BF16)
        for h in range(MLA_HEADS):
            kcat_o[:, h * 2 * LANES: h * 2 * LANES + LANES] = kv[:, h * MLA_D_NOPE:(h + 1) * MLA_D_NOPE].astype(BF16)
            kcat_o[:, h * 2 * LANES + LANES:(h + 1) * 2 * LANES] = krb

    for h in range(DSA_HEADS):
        x = _rms(c_ref[:, C_DQ + h * LANES: C_DQ + (h + 1) * LANES], gdq_ref[...])
        dq_o[:, h * LANES:(h + 1) * LANES] = (_rope3(x, rt_ref, RT_DSA, DSA_ROT // 2) * DSA_SCALE).astype(BF16)
    for g in range(DSA_KV_HEADS):
        x = _rms(c_ref[:, C_DK + g * LANES: C_DK + (g + 1) * LANES], gdk_ref[...])
        x = _rope3(x, rt_ref, RT_DSA, DSA_ROT // 2)
        dk_o[:, g * LANES:(g + 1) * LANES] = x
        if not sample:
            dkb_o[:, g * LANES:(g + 1) * LANES] = x.astype(BF16)
    x = c_ref[:, C_DV:C_DV + DSA_KV_HEADS * DSA_HEAD_DIM]
    dv_o[...] = x
    if not sample:
        dvb_o[...] = x.astype(BF16)

    for p in range(IDX_HEADS // 2):
        x = c_ref[:, C_IQ + p * LANES: C_IQ + (p + 1) * LANES]
        iq_o[:, p * LANES:(p + 1) * LANES] = _rope3(x, rt_ref, RT_IDX, IDX_ROT // 2).astype(BF16)
    iw_o[...] = c_ref[:, C_IW:C_IW + LANES]

    for h in range(MEM_HEADS):
        x = _rms(c_ref[:, C_MQ + h * MEM_HEAD_DIM: C_MQ + (h + 1) * MEM_HEAD_DIM], gmq_ref[...])
        mq_o[:, h * MEM_HEAD_DIM:(h + 1) * MEM_HEAD_DIM] = (x * MEM_SCALE).astype(BF16)


def _epilogue(c, rt, gains, wuq, wx, sample):
    T = c.shape[0]
    tm = min(T, 256)
    n_rt = rt.shape[0] // tm
    row = lambda n: pl.BlockSpec((tm, n), lambda i: (i, 0))
    full = lambda a: pl.BlockSpec(a.shape, lambda i: (0,) * a.ndim)
    widths_f32 = [MLA_D_CKV, MLA_D_ROPE, 2 * LANES, 2 * LANES, IDX_DIM, LANES]
    if sample:
        widths_b16 = [MLA_HEADS * MLA_D_CKV, MLA_HEADS * MLA_D_ROPE, 8 * LANES, 8 * LANES, 8 * LANES]
    else:
        widths_b16 = [16 * LANES, 16 * LANES, 8 * LANES, 8 * LANES, 2 * LANES, 2 * LANES, 8 * LANES, IDX_DIM, 8 * LANES]
    out_shape = ([jax.ShapeDtypeStruct((T, n), F32) for n in widths_f32]
                 + [jax.ShapeDtypeStruct((T, n), BF16) for n in widths_b16])
    return pl.pallas_call(
        functools.partial(_epilogue_kernel, sample=sample),
        grid=(T // tm,),
        in_specs=[pl.BlockSpec((tm, C_GATE), lambda i: (i, 0)),
                  pl.BlockSpec((tm, RT_SLOTS * LANES), lambda i: (i % n_rt, 0))]
                 + [full(g) for g in gains] + [full(wuq), full(wx)],
        out_specs=[row(n) for n in widths_f32 + widths_b16],
        out_shape=out_shape,
        compiler_params=_cparams("parallel"),
    )(c, rt, *gains, wuq, wx)


def _mla_flash_kernel(q_ref, k_ref, v_ref, o_ref, m_ref, l_ref, acc_ref):
    i, j = pl.program_id(1), pl.program_id(2)
    tq, tk = q_ref.shape[0], k_ref.shape[0]
    qk, dv = 2 * LANES, MLA_D_V

    @pl.when(j == 0)
    def _():
        _softmax_init(m_ref, l_ref, acc_ref)

    def step(mask):
        for h0 in range(0, MLA_HEADS, FLASH_HEAD_GROUP):
            heads = range(h0, h0 + FLASH_HEAD_GROUP)
            scores = [_dot_nt(q_ref[:, h * qk:(h + 1) * qk], k_ref[:, h * qk:(h + 1) * qk]) for h in heads]
            if mask is not None:
                scores = [jnp.where(mask, s, NEG) for s in scores]
            _softmax_updates(scores, [v_ref[:, h * dv:(h + 1) * dv] for h in heads],
                             [(m_ref.at[h], l_ref.at[h], acc_ref.at[h]) for h in heads])

    @pl.when(j < i)
    def _():
        step(None)

    @pl.when(j == i)
    def _():
        step(lax.broadcasted_iota(I32, (tq, tk), 1) <= lax.broadcasted_iota(I32, (tq, tk), 0))
        for h in range(MLA_HEADS):
            o_ref[:, h * dv:(h + 1) * dv] = (acc_ref[h] / l_ref[h]).astype(o_ref.dtype)


def _mla_prompt(qcat, kcat, vb, B, T):
    tq = min(T, 512)
    nq = T // tq
    H = MLA_HEADS
    return pl.pallas_call(
        _mla_flash_kernel,
        grid=(B, nq, nq),
        in_specs=[
            pl.BlockSpec((tq, H * 2 * LANES), lambda b, i, j: (b * nq + i, 0)),
            pl.BlockSpec((tq, H * 2 * LANES), lambda b, i, j: (b * nq + jnp.minimum(i, j), 0)),
            pl.BlockSpec((tq, H * MLA_D_V), lambda b, i, j: (b * nq + jnp.minimum(i, j), 0)),
        ],
        out_specs=pl.BlockSpec((tq, H * MLA_D_V), lambda b, i, j: (b * nq + i, 0)),
        out_shape=jax.ShapeDtypeStruct((B * T, H * MLA_D_V), BF16),
        scratch_shapes=[pltpu.VMEM((H, tq, LANES), F32), pltpu.VMEM((H, tq, LANES), F32),
                        pltpu.VMEM((H, tq, MLA_D_V), F32)],
        compiler_params=_cparams("parallel", "parallel", "arbitrary"),
    )(qcat, kcat, vb)


def _select_threshold(count_fn, shape, ksel, n_index_bits):
    def value_step(s, prefix):
        cand = prefix | lax.shift_left(jnp.int32(1), 31 - s)
        cand_f = _ordinal_to_float(cand)
        n = count_fn(lambda x, col: x >= cand_f)
        return jnp.where(n >= ksel, cand, prefix)

    thr = _ordinal_to_float(lax.fori_loop(0, 32, value_step, jnp.zeros(shape, I32)))
    all_visible = count_fn(lambda x, col: x > -jnp.inf) <= ksel
    need = ksel - count_fn(lambda x, col: x > thr)
    n_ge = count_fn(lambda x, col: x >= thr)

    def search_cut():
        def cut_step(s, cut):
            cand = cut | lax.shift_left(jnp.int32(1), n_index_bits - 1 - s)
            n = count_fn(lambda x, col: (x == thr) & (col < cand))
            return jnp.where(n <= need, cand, cut)
        return lax.fori_loop(0, n_index_bits, cut_step, jnp.zeros(shape, I32))

    cut = lax.cond(jnp.max(n_ge) > ksel, search_cut, lambda: jnp.full(shape, 2 ** n_index_bits, I32))
    return all_visible, thr, cut


def _dsa_prompt_kernel(iq_ref, iw_ref, ik_ref, dq_ref, dk_ref, dv_ref, o_ref,
                       key_ref, bias_ref, m_ref, l_ref, acc_ref, *, ck, ksel):
    i = pl.program_id(1)
    tq = dq_ref.shape[0]
    T = ik_ref.shape[0]
    nch = ((i + 1) * tq + ck - 1) // ck
    row = i * tq + lax.broadcasted_iota(I32, (tq, ck), 0)
    col0 = lax.broadcasted_iota(I32, (tq, ck), 1)

    def idx_chunk(c, carry):
        ikc = ik_ref[pl.ds(pl.multiple_of(c * ck, ck), ck), :]
        acc = jnp.zeros((tq, ck), F32)
        for h in range(IDX_HEADS):
            acc = acc + jnp.maximum(_dot_nt(iq_ref[h], ikc), 0.0) * iw_ref[:, h:h + 1]
        key_ref[c] = jnp.where(c * ck + col0 <= row, acc, -jnp.inf)
        return carry

    lax.fori_loop(0, nch, idx_chunk, 0)

    def count_fn(pred):
        def body(c, a):
            hit = jnp.where(pred(key_ref[c], c * ck + col0), 1.0, 0.0)
            for t in range(ck // LANES):
                a = a + hit[:, t * LANES:(t + 1) * LANES]
            return a
        a = lax.fori_loop(0, nch, body, jnp.zeros((tq, LANES), F32))
        return jnp.sum(a, axis=1, keepdims=True)

    all_visible, thr, cut = _select_threshold(count_fn, (tq, 1), ksel, T.bit_length())

    def bias_chunk(c, carry):
        x = key_ref[c]
        col = c * ck + col0
        sel = (col <= row) & (all_visible | (x > thr) | ((x == thr) & (col < cut)))
        bias_ref[c] = jnp.where(sel, 0.0, NEG)
        return carry

    lax.fori_loop(0, nch, bias_chunk, 0)

    qg = [jnp.concatenate([dq_ref[:, (g * DSA_GROUP + r) * LANES:(g * DSA_GROUP + r + 1) * LANES]
                           for r in range(DSA_GROUP)], axis=0) for g in range(DSA_KV_HEADS)]
    _softmax_init(m_ref, l_ref, acc_ref)

    def attn_chunk(c, carry):
        k0 = pl.multiple_of(c * ck, ck)
        bias = jnp.concatenate([bias_ref[c]] * DSA_GROUP, axis=0)
        heads = range(DSA_KV_HEADS)
        scores = [_dot_nt(qg[g], dk_ref[pl.ds(k0, ck), g * LANES:(g + 1) * LANES]) + bias for g in heads]
        _softmax_updates(scores, [dv_ref[pl.ds(k0, ck), g * LANES:(g + 1) * LANES] for g in heads],
                         [(m_ref.at[g], l_ref.at[g], acc_ref.at[g]) for g in heads])
        return carry

    lax.fori_loop(0, nch, attn_chunk, 0)
    for g in range(DSA_KV_HEADS):
        o = acc_ref[g] / l_ref[g]
        for r in range(DSA_GROUP):
            o_ref[:, (g * DSA_GROUP + r) * LANES:(g * DSA_GROUP + r + 1) * LANES] = (
                o[r * tq:(r + 1) * tq].astype(o_ref.dtype))


def _dsa_prompt(iq3, iw, ikb, dq, dkb, dvb, B, T):
    tq = min(T, 256)
    ck = min(T, 512)
    nq = T // tq
    ksel = min(TOPK_MAX, T // 4)
    return pl.pallas_call(
        functools.partial(_dsa_prompt_kernel, ck=ck, ksel=ksel),
        grid=(B, nq),
        in_specs=[
            pl.BlockSpec((IDX_HEADS, tq, IDX_DIM), lambda b, i: (0, b * nq + i, 0)),
            pl.BlockSpec((tq, LANES), lambda b, i: (b * nq + i, 0)),
            pl.BlockSpec((T, IDX_DIM), lambda b, i: (b, 0)),
            pl.BlockSpec((tq, DSA_HEADS * LANES), lambda b, i: (b * nq + i, 0)),
            pl.BlockSpec((T, DSA_KV_HEADS * LANES), lambda b, i: (b, 0)),
            pl.BlockSpec((T, DSA_KV_HEADS * LANES), lambda b, i: (b, 0)),
        ],
        out_specs=pl.BlockSpec((tq, DSA_HEADS * LANES), lambda b, i: (b * nq + i, 0)),
        out_shape=jax.ShapeDtypeStruct((B * T, DSA_HEADS * LANES), BF16),
        scratch_shapes=[pltpu.VMEM((T // ck, tq, ck), F32), pltpu.VMEM((T // ck, tq, ck), F32)]
                       + [pltpu.VMEM((DSA_KV_HEADS, DSA_GROUP * tq, LANES), F32)] * 3,
        compiler_params=_cparams("parallel", "arbitrary"),
    )(iq3, iw, ikb, dq, dkb, dvb)


def _memkv_kernel(x_ref, g_ref, w_ref, gk_ref, k_ref, v_ref):
    h = _rms(x_ref[...], g_ref[...]).astype(BF16)
    kv = _dot(h, w_ref[...])
    n = MEM_HEADS * MEM_HEAD_DIM
    for hd in range(MEM_HEADS):
        k_ref[:, hd * MEM_HEAD_DIM:(hd + 1) * MEM_HEAD_DIM] = _rms(
            kv[:, hd * MEM_HEAD_DIM:(hd + 1) * MEM_HEAD_DIM], gk_ref[...])
    v_ref[...] = kv[:, n:]


def _memory_kv(mem, g_in, w, g_k):
    T, D = mem.shape
    n = MEM_HEADS * MEM_HEAD_DIM
    tm = min(T, 256)
    return pl.pallas_call(
        _memkv_kernel,
        grid=(T // tm,),
        in_specs=[pl.BlockSpec((tm, D), lambda i: (i, 0)), pl.BlockSpec((1, D), lambda i: (0, 0)),
                  pl.BlockSpec((D, 2 * n), lambda i: (0, 0)), pl.BlockSpec((1, MEM_HEAD_DIM), lambda i: (0, 0))],
        out_specs=[pl.BlockSpec((tm, n), lambda i: (i, 0))] * 2,
        out_shape=[jax.ShapeDtypeStruct((T, n), F32)] * 2,
        compiler_params=_cparams("parallel"),
    )(mem, g_in, w, g_k)


def _mem_attend_kernel(q_ref, k_ref, v_ref, o_ref):
    heads = [slice(h * MEM_HEAD_DIM, (h + 1) * MEM_HEAD_DIM) for h in range(MEM_HEADS)]
    scores = [_dot_nt(q_ref[:, sl], k_ref[:, sl]) for sl in heads]
    probs = []
    for s in scores:
        p = jnp.exp(s - jnp.max(s, axis=1, keepdims=True))
        probs.append((p / jnp.sum(p, axis=1, keepdims=True)).astype(BF16))
    for p, sl in zip(probs, heads):
        o_ref[:, sl] = _dot(p, v_ref[:, sl]).astype(o_ref.dtype)


def _mem_attend(mq, mk, mv):
    G, R, n = mq.shape
    M = mk.shape[1]
    tm = min(R, 512)
    mem = pl.BlockSpec((None, M, n), lambda g, i: (g, 0, 0))
    return pl.pallas_call(
        _mem_attend_kernel,
        grid=(G, R // tm),
        in_specs=[pl.BlockSpec((None, tm, n), lambda g, i: (g, i, 0)), mem, mem],
        out_specs=pl.BlockSpec((None, tm, n), lambda g, i: (g, i, 0)),
        out_shape=jax.ShapeDtypeStruct((G, R, n), BF16),
        compiler_params=_cparams("parallel", "parallel"),
    )(mq, mk, mv)


def _merge_kernel(x_ref, oa_ref, ob_ref, oc_ref, ga_ref, gb_ref, gc_ref, wa_ref, wb_ref, wc_ref, wo_ref, o_ref):
    @pl.when(pl.program_id(1) == 0)
    def _():
        o_ref[...] = x_ref[...]

    merged = (jax.nn.sigmoid(ga_ref[...]) * _dot(oa_ref[...], wa_ref[...])
              + jax.nn.sigmoid(gb_ref[...]) * _dot(ob_ref[...], wb_ref[...])
              + jax.nn.sigmoid(gc_ref[...]) * _dot(oc_ref[...], wc_ref[...]))
    o_ref[...] += _dot(merged.astype(BF16), wo_ref[...])


def _merge(x, oa, ob, oc, c, wa, wb, wc, wo):
    T, D = x.shape
    n = oa.shape[1]
    tm = min(T, 512)
    tn = min(D, 512)
    g0 = C_GATE // tn
    nb = D // tn
    gate = lambda k: pl.BlockSpec((tm, tn), lambda i, j: (i, g0 + k * nb + j))
    branch_w = pl.BlockSpec((n, tn), lambda i, j: (0, j))
    rows = pl.BlockSpec((tm, n), lambda i, j: (i, 0))
    return pl.pallas_call(
        _merge_kernel,
        grid=(T // tm, nb),
        in_specs=[pl.BlockSpec((tm, D), lambda i, j: (i, 0)), rows, rows, rows, gate(0), gate(1), gate(2),
                  branch_w, branch_w, branch_w, pl.BlockSpec((tn, D), lambda i, j: (j, 0))],
        out_specs=pl.BlockSpec((tm, D), lambda i, j: (i, 0)),
        out_shape=jax.ShapeDtypeStruct((T, D), F32),
        compiler_params=_cparams("parallel", "arbitrary"),
    )(x, oa, ob, oc, c, c, c, wa, wb, wc, wo)


def _sample_scores_kernel(pt_ref, iq_ref, iw_ref, ikn_ref, *rest, n_pages, group):
    page_refs, o_ref = rest[:n_pages], rest[n_pages]
    R = iq_ref.shape[0]
    TP = R // IDX_HEADS
    iq = iq_ref[...]
    iw = iw_ref[...]

    def head_sum(s):
        s = jnp.maximum(s, 0.0) * iw
        return jnp.sum(s.reshape(TP, IDX_HEADS, s.shape[1]), axis=1)

    for g0 in range(0, n_pages, group):
        kt = jnp.concatenate([page_refs[k][...].astype(BF16) for k in range(g0, g0 + group)], axis=1)
        o_ref[:, g0 * PAGE_SIZE:(g0 + group) * PAGE_SIZE] = head_sum(_dot(iq, kt))
    o_ref[:, n_pages * PAGE_SIZE:] = head_sum(_dot_nt(iq, ikn_ref[...].astype(BF16)))


def _sample_scores(pt, iq, iw, ik_new, cache_ikt, n_pages):
    DB, R, _ = iq.shape
    TP = R // IDX_HEADS
    W = (n_pages + 1) * PAGE_SIZE

    def page_spec(k):
        return pl.BlockSpec((None, IDX_DIM, PAGE_SIZE), lambda b, pt: (pt[b * n_pages + k], 0, 0))

    grid_spec = pltpu.PrefetchScalarGridSpec(
        num_scalar_prefetch=1,
        grid=(DB,),
        in_specs=[pl.BlockSpec((None, R, IDX_DIM), lambda b, pt: (b, 0, 0)),
                  pl.BlockSpec((None, R, 1), lambda b, pt: (b, 0, 0)),
                  pl.BlockSpec((None, PAGE_SIZE, IDX_DIM), lambda b, pt: (b, 0, 0))]
                 + [page_spec(k) for k in range(n_pages)],
        out_specs=pl.BlockSpec((None, TP, W), lambda b, pt: (b, 0, 0)),
    )
    return pl.pallas_call(
        functools.partial(_sample_scores_kernel, n_pages=n_pages, group=min(n_pages, 8)),
        grid_spec=grid_spec,
        out_shape=jax.ShapeDtypeStruct((DB, TP, W), F32),
        compiler_params=_cparams("parallel"),
    )(pt, iq, iw, ik_new, *([cache_ikt] * n_pages))


def _sample_select_kernel(s_ref, o_ref, *, past, ksel, tp):
    RB, W = s_ref.shape
    t = lax.broadcasted_iota(I32, (RB, W), 0) % tp
    col = lax.broadcasted_iota(I32, (RB, W), 1)
    visible = col <= past + t
    x = jnp.where(visible, s_ref[...], -jnp.inf)

    def count_fn(pred):
        hit = jnp.where(pred(x, col), 1.0, 0.0)
        a = hit[:, :LANES]
        for c in range(1, W // LANES):
            a = a + hit[:, c * LANES:(c + 1) * LANES]
        return jnp.sum(a, axis=1, keepdims=True)

    all_visible, thr, cut = _select_threshold(count_fn, (RB, 1), ksel, W.bit_length())
    sel = visible & (all_visible | (x > thr) | ((x == thr) & (col < cut)))
    o_ref[...] = jnp.where(sel, 1.0, 0.0).astype(o_ref.dtype)


def _sample_select(scores, past, ksel, tp):
    R, W = scores.shape
    rb = min(R, 64)
    return pl.pallas_call(
        functools.partial(_sample_select_kernel, past=past, ksel=ksel, tp=tp),
        grid=(R // rb,),
        in_specs=[pl.BlockSpec((rb, W), lambda i: (i, 0))],
        out_specs=pl.BlockSpec((rb, W), lambda i: (i, 0)),
        out_shape=jax.ShapeDtypeStruct((R, W), BF16),
        compiler_params=_cparams("parallel"),
    )(scores)


def _sample_attend_kernel(pt_ref, qlat_ref, qr_ref, dq_ref, sel_ref, seln_ref,
                          ckvn_ref, krn_ref, dkn_ref, dvn_ref, ckv_hbm, krt_hbm, dk_hbm, dv_hbm,
                          olat_ref, odsa_ref, ma_ref, la_ref, acca_ref, md_ref, ld_ref, accd_ref,
                          ckv_cat, krt_cat, dk_cat, dv_cat, ckv_buf, krt_buf, dk_buf, dv_buf, sem, *, pages, n_steps):
    b, j = pl.program_id(0), pl.program_id(1)
    step = b * n_steps + j
    n_total = pl.num_programs(0) * n_steps
    slot = step % 2
    RA, RD = qlat_ref.shape[0], dq_ref.shape[0]
    states = [(ma_ref, la_ref, acca_ref), (md_ref, ld_ref, accd_ref)]

    def page_copies(step, slot):
        copies = []
        for k in range(pages):
            page = pt_ref[step * pages + k]
            copies += [pltpu.make_async_copy(hbm.at[page], buf.at[slot, k], sem.at[slot, a])
                       for a, (hbm, buf) in enumerate(((ckv_hbm, ckv_buf), (krt_hbm, krt_buf),
                                                       (dk_hbm, dk_buf), (dv_hbm, dv_buf)))]
        return copies

    @pl.when(step == 0)
    def _():
        for c in page_copies(0, 0):
            c.start()

    @pl.when(step + 1 < n_total)
    def _():
        for c in page_copies(step + 1, 1 - slot):
            c.start()

    for c in page_copies(step, slot):
        c.wait()
    ckv_pages, krt_pages, dk_pages, dv_pages = ([buf.at[slot, k] for k in range(pages)]
                                                for buf in (ckv_buf, krt_buf, dk_buf, dv_buf))

    @pl.when(j == 0)
    def _():
        _softmax_init(ma_ref, la_ref, acca_ref)
        _softmax_init(md_ref, ld_ref, accd_ref)

    qlat, qr, dq = qlat_ref[...], qr_ref[...], dq_ref[...]

    def dsa_mask(flags_ref, s):
        flags = flags_ref[...].astype(F32)
        selected = jnp.concatenate([flags] * (RD // flags.shape[0]), axis=0) > 0.5
        head_of_col = lax.broadcasted_iota(I32, selected.shape, 1) % DSA_KV_HEADS
        head_of_row = lax.broadcasted_iota(I32, selected.shape, 0) // (RD // DSA_KV_HEADS)
        return jnp.where(selected & (head_of_col == head_of_row), s, NEG)

    s_mla, s_dsa = [], []
    for c0 in range(0, pages, SAMPLE_CHUNK_PAGES):
        for k in range(c0, min(c0 + SAMPLE_CHUNK_PAGES, pages)):
            ckv_cat[k * PAGE_SIZE:(k + 1) * PAGE_SIZE, :] = ckv_pages[k][...].astype(BF16)
            krt_cat[:, k * PAGE_SIZE:(k + 1) * PAGE_SIZE] = krt_pages[k][...].astype(BF16)
            dk_cat[k * 2 * PAGE_SIZE:(k + 1) * 2 * PAGE_SIZE, :] = dk_pages[k][...].astype(BF16)
            dv_cat[k * 2 * PAGE_SIZE:(k + 1) * 2 * PAGE_SIZE, :] = dv_pages[k][...].astype(BF16)
        keys = slice(c0 * PAGE_SIZE, min(c0 + SAMPLE_CHUNK_PAGES, pages) * PAGE_SIZE)
        rows = slice(2 * keys.start, 2 * keys.stop)
        s_mla.append(_dot_nt(qlat, ckv_cat[keys, :]) + _dot(qr, krt_cat[:, keys]))
        s_dsa.append(_dot_nt(dq, dk_cat[rows, :]))
    _softmax_updates([jnp.concatenate(s_mla, axis=1), dsa_mask(sel_ref, jnp.concatenate(s_dsa, axis=1))],
                     [ckv_cat[...], dv_cat[...]], states)

    @pl.when(j == n_steps - 1)
    def _():
        t = lax.broadcasted_iota(I32, (RA, PAGE_SIZE), 0) // MLA_HEADS
        cn = lax.broadcasted_iota(I32, (RA, PAGE_SIZE), 1)
        ckvn = ckvn_ref[...].astype(BF16)
        s = _dot_nt(qlat, ckvn) + _dot_nt(qr, krn_ref[...].astype(BF16))
        sd = dsa_mask(seln_ref, _dot_nt(dq, dkn_ref[...].astype(BF16)))
        _softmax_updates([jnp.where(cn <= t, s, NEG), sd], [ckvn, dvn_ref[...].astype(BF16)], states)
        olat_ref[...] = acca_ref[...] / _lane_tile(la_ref[...], MLA_D_CKV)
        odsa_ref[...] = accd_ref[...] / ld_ref[...]


def _sample_attend(pt, qlat, qr, dq, sel, sel_new, new_rows, caches, n_pages):
    DB, RA, _ = qlat.shape
    RD = dq.shape[1]
    TP = sel.shape[1]
    pages = min(n_pages, SAMPLE_PAGES_PER_STEP)
    n_steps = n_pages // pages
    page_specs = [pl.BlockSpec(memory_space=pl.ANY)] * len(caches)
    per_seq = lambda shape: pl.BlockSpec((None,) + shape, lambda b, j, pt: (b,) + (0,) * len(shape))
    in_specs = [per_seq((RA, MLA_D_CKV)), per_seq((RA, MLA_D_ROPE)), per_seq((RD, LANES)),
                pl.BlockSpec((None, TP, pages * 2 * PAGE_SIZE), lambda b, j, pt: (b, 0, j)),
                per_seq((TP, 2 * PAGE_SIZE)),
                per_seq((PAGE_SIZE, MLA_D_CKV)), per_seq((PAGE_SIZE, MLA_D_ROPE)),
                per_seq((2 * PAGE_SIZE, LANES)), per_seq((2 * PAGE_SIZE, LANES))]
    n_keys = pages * PAGE_SIZE
    grid_spec = pltpu.PrefetchScalarGridSpec(
        num_scalar_prefetch=1,
        grid=(DB, n_steps),
        in_specs=in_specs + page_specs,
        out_specs=[per_seq((RA, MLA_D_CKV)), per_seq((RD, LANES))],
        scratch_shapes=[pltpu.VMEM((RA, LANES), F32), pltpu.VMEM((RA, LANES), F32), pltpu.VMEM((RA, MLA_D_CKV), F32)]
                       + [pltpu.VMEM((RD, LANES), F32)] * 3
                       + [pltpu.VMEM((n_keys, MLA_D_CKV), BF16), pltpu.VMEM((MLA_D_ROPE, n_keys), BF16),
                          pltpu.VMEM((2 * n_keys, LANES), BF16), pltpu.VMEM((2 * n_keys, LANES), BF16)]
                       + [pltpu.VMEM((2, pages) + c.shape[1:], c.dtype) for c in caches]
                       + [pltpu.SemaphoreType.DMA((2, len(caches)))],
    )
    return pl.pallas_call(
        functools.partial(_sample_attend_kernel, pages=pages, n_steps=n_steps),
        grid_spec=grid_spec,
        out_shape=[jax.ShapeDtypeStruct((DB, RA, MLA_D_CKV), F32), jax.ShapeDtypeStruct((DB, RD, LANES), F32)],
        compiler_params=_cparams("arbitrary", "arbitrary"),
    )(pt, qlat, qr, dq, sel, sel_new, *new_rows, *caches)


def _uv_kernel(o_ref, w_ref, out_ref):
    out_ref[...] = _dot(o_ref[...], w_ref[...]).astype(out_ref.dtype)


def _mla_up_v(olat_h, wuv_t):
    H, T, C = olat_h.shape
    return pl.pallas_call(
        _uv_kernel,
        grid=(H,),
        in_specs=[pl.BlockSpec((None, T, C), lambda h: (h, 0, 0)), pl.BlockSpec((None, C, MLA_D_V), lambda h: (h, 0, 0))],
        out_specs=pl.BlockSpec((T, MLA_D_V), lambda h: (0, h)),
        out_shape=jax.ShapeDtypeStruct((T, H * MLA_D_V), BF16),
        compiler_params=_cparams("parallel"),
    )(olat_h, wuv_t)


def _rope_tables(pos):
    pos = pos.astype(F32)
    T = pos.shape[0]

    def tabs(rot, period):
        half = rot // 2
        inv_freq = ROPE_THETA ** (-jnp.arange(half, dtype=F32) / half)
        ang = pos[:, None] * inv_freq[None, :]
        cos, sin = jnp.cos(ang), jnp.sin(ang)
        one, zero = jnp.ones((T, period - rot), F32), jnp.zeros((T, period - rot), F32)
        zh = jnp.zeros((T, half), F32)
        return (jnp.concatenate([cos, cos, one], 1), jnp.concatenate([-sin, zh, zero], 1),
                jnp.concatenate([zh, sin, zero], 1))

    rep = lambda ts, period: [jnp.tile(t, (1, LANES // period)) for t in ts]
    mla, idx = tabs(MLA_D_ROPE, MLA_D_ROPE), tabs(IDX_ROT, IDX_DIM)
    z = jnp.zeros((T, LANES // 2), F32)
    kri = [jnp.concatenate([mla[0], idx[0]], 1), jnp.concatenate([mla[1], z], 1), jnp.concatenate([mla[2], z], 1),
           jnp.concatenate([z, idx[1]], 1), jnp.concatenate([z, idx[2]], 1)]
    return jnp.concatenate(rep(mla, MLA_D_ROPE) + rep(tabs(DSA_ROT, LANES), LANES) + rep(idx, IDX_DIM) + kri, axis=1)


def _permute_w_in(w_in):
    offs, off = {}, 0
    for name, w in (("mla_cq", MLA_D_CQ), ("mla_ckv", MLA_D_CKV), ("mla_kr", MLA_D_ROPE),
                    ("dsa_q", DSA_HEADS * DSA_HEAD_DIM), ("dsa_k", DSA_KV_HEADS * DSA_HEAD_DIM),
                    ("dsa_v", DSA_KV_HEADS * DSA_HEAD_DIM), ("idx_q", IDX_HEADS * IDX_DIM), ("idx_w", IDX_HEADS),
                    ("idx_k", IDX_DIM), ("mem_q", MEM_HEADS * MEM_HEAD_DIM), ("gates", N_BRANCH * D_MODEL)):
        offs[name] = (off, w)
        off += w
    w_t = jnp.swapaxes(w_in, 0, 1)
    col = lambda n: w_t[offs[n][0]: offs[n][0] + offs[n][1]]
    pad = jnp.zeros((LANES - IDX_HEADS, w_in.shape[0]), w_in.dtype)
    return jnp.concatenate([col("mla_cq"), col("mla_ckv"), col("dsa_q"), col("dsa_k"), col("dsa_v"), col("idx_q"),
                            col("mem_q"), col("mla_kr"), col("idx_k"), col("idx_w"), pad, col("gates")], axis=0)


def kernel(x_prompt, x_sample, cache_mla_ckv, cache_mla_krope, cache_dsa_k, cache_dsa_v, cache_idx_k, cache_mem_k,
           cache_mem_v, page_table, mem_prompt, g_ffn1, w_ffn1_gate, w_ffn1_up, w_ffn1_down, g_mix, w_in, g_mla_cq,
           w_mla_uq, g_mla_q, w_mla_uk, w_mla_uv, g_mla_ckv, g_mla_kr, g_dsa_q, g_dsa_k, g_mem_in, w_mem_kv,
           g_mem_q, g_mem_k, w_o_mla, w_o_dsa, w_o_mem, w_out, g_ffn2, w_ffn2_gate, w_ffn2_up, w_ffn2_down):
    B, S, D = x_prompt.shape
    DB, TS, _ = x_sample.shape
    depth = g_mix.shape[0]
    n_pages = page_table.shape[1]
    past = n_pages * PAGE_SIZE
    TP = SUBLANES
    xp = x_prompt.reshape(B * S, D)
    xs = x_sample.reshape(DB * TS, D)
    pt = page_table.reshape(-1)
    rt_p = _rope_tables(jnp.arange(S))
    rt_s = jnp.tile(_rope_tables(past + jnp.arange(TS)), (DB, 1))
    row = lambda g: g.reshape(1, -1).astype(F32)
    b16 = lambda w: w.astype(BF16)
    st = {k: [] for k in ("ckv_p", "kr_p", "dk_p", "dv_p", "ik_p", "mk_p", "mv_p", "ckv_s", "kr_s", "dk_s", "dv_s", "ik_s")}

    for l in range(depth):
        ffn1 = (row(g_ffn1[l]), b16(w_ffn1_gate[l]), b16(w_ffn1_up[l]), b16(w_ffn1_down[l]))
        ffn2 = (row(g_ffn2[l]), b16(w_ffn2_gate[l]), b16(w_ffn2_up[l]), b16(w_ffn2_down[l]))
        w_in_p = b16(_permute_w_in(w_in[l]))
        gq = g_mla_q[l]
        gains = (row(g_mla_cq[l]), row(gq[:MLA_D_NOPE]), row(jnp.tile(gq[MLA_D_NOPE:], 2)), row(g_mla_ckv[l]),
                 row(jnp.concatenate([g_mla_kr[l], jnp.ones((LANES - MLA_D_ROPE,), F32)])),
                 row(g_dsa_q[l]), row(g_dsa_k[l]), row(g_mem_q[l]))
        wuq = w_mla_uq[l].reshape(MLA_D_CQ, MLA_HEADS, MLA_D_QK)
        wuq = b16(jnp.concatenate([wuq[:, :, :MLA_D_NOPE].reshape(MLA_D_CQ, -1),
                                   wuq[:, :, MLA_D_NOPE:].reshape(MLA_D_CQ, -1)], axis=1))
        wuk, wuv = w_mla_uk[l], w_mla_uv[l]
        w_kv = b16(jnp.concatenate([wuk.transpose(2, 0, 1).reshape(MLA_D_CKV, -1),
                                    wuv.transpose(2, 0, 1).reshape(MLA_D_CKV, -1)], axis=1))
        merge_w = (b16(w_o_mla[l]), b16(w_o_dsa[l]), b16(w_o_mem[l]), b16(w_out[l]))

        mk_p, mv_p = _memory_kv(mem_prompt.reshape(-1, D), row(g_mem_in[l]), b16(w_mem_kv[l]), row(g_mem_k[l]))
        M = mem_prompt.shape[1]
        xp = _ffn_half(xp, *ffn1)
        c = _project(xp, row(g_mix[l]), w_in_p)
        (ckv, kr, dk, dv, ik, iw, qcat, kcat, vb, dq, dkb, dvb, iq, ikb, mq) = _epilogue(
            c, rt_p, gains, wuq, w_kv, sample=False)
        o_mla = _mla_prompt(qcat, kcat, vb, B, S)
        iq3 = iq.reshape(B * S, IDX_HEADS, IDX_DIM).transpose(1, 0, 2)
        o_dsa = _dsa_prompt(iq3, iw, ikb, dq, dkb, dvb, B, S)
        o_mem = _mem_attend(mq.reshape(B, S, -1), b16(mk_p).reshape(B, M, -1),
                            b16(mv_p).reshape(B, M, -1)).reshape(B * S, -1)
        xp = _merge(xp, o_mla, o_dsa, o_mem, c, *merge_w)
        xp = _ffn_half(xp, *ffn2)
        st["ckv_p"].append(ckv.reshape(B, S, MLA_D_CKV))
        st["kr_p"].append(kr.reshape(B, S, MLA_D_ROPE))
        st["dk_p"].append(dk.reshape(B, S, DSA_KV_HEADS, DSA_HEAD_DIM))
        st["dv_p"].append(dv.reshape(B, S, DSA_KV_HEADS, DSA_HEAD_DIM))
        st["ik_p"].append(ik.reshape(B, S, IDX_DIM))
        st["mk_p"].append(mk_p.reshape(B, M, MEM_HEADS, MEM_HEAD_DIM))
        st["mv_p"].append(mv_p.reshape(B, M, MEM_HEADS, MEM_HEAD_DIM))

        xs = _ffn_half(xs, *ffn1)
        c = _project(xs, row(g_mix[l]), w_in_p)
        (ckv, kr, dk, dv, ik, iw, qlat, qr, dq, iq, mq) = _epilogue(c, rt_s, gains, wuq, b16(wuk), sample=True)
        pad_t = lambda a: jnp.pad(a.reshape(DB, TS, -1), ((0, 0), (0, PAGE_SIZE - TS), (0, 0)))
        iq_s = jnp.pad(iq.reshape(DB, TS, IDX_HEADS, IDX_DIM), ((0, 0), (0, TP - TS), (0, 0), (0, 0)))
        iw_s = jnp.pad(iw[:, :IDX_HEADS].reshape(DB, TS, IDX_HEADS), ((0, 0), (0, TP - TS), (0, 0)))
        ksel = min(TOPK_MAX, (past + TS) // 4)
        scores = _sample_scores(pt, iq_s.reshape(DB, TP * IDX_HEADS, IDX_DIM), iw_s.reshape(DB, TP * IDX_HEADS, 1),
                                pad_t(ik), jnp.swapaxes(cache_idx_k[l], 1, 2), n_pages)
        sel = _sample_select(scores.reshape(DB * TP, -1), past, ksel, TP).reshape(DB, TP, -1)
        dq_s = jnp.pad(dq.reshape(DB, TS, DSA_KV_HEADS, DSA_GROUP, LANES).transpose(0, 2, 3, 1, 4),
                       ((0, 0), (0, 0), (0, 0), (0, TP - TS), (0, 0))).reshape(DB, DSA_HEADS * TP, LANES)
        sel2 = jnp.repeat(sel, DSA_KV_HEADS, axis=2)
        as_page = lambda a: pad_t(a).reshape(DB, DSA_KV_HEADS * PAGE_SIZE, LANES)
        olat, odsa = _sample_attend(
            pt, qlat.reshape(DB, TS * MLA_HEADS, MLA_D_CKV), qr.reshape(DB, TS * MLA_HEADS, MLA_D_ROPE), dq_s,
            sel2[:, :, :DSA_KV_HEADS * past], sel2[:, :, DSA_KV_HEADS * past:],
            (pad_t(ckv), pad_t(kr), as_page(dk), as_page(dv)),
            (cache_mla_ckv[l], jnp.swapaxes(cache_mla_krope[l], 1, 2),
             cache_dsa_k[l].reshape(-1, DSA_KV_HEADS * PAGE_SIZE, LANES),
             cache_dsa_v[l].reshape(-1, DSA_KV_HEADS * PAGE_SIZE, LANES)), n_pages)
        olat_h = b16(olat.reshape(DB * TS, MLA_HEADS, MLA_D_CKV).transpose(1, 0, 2))
        o_mla = _mla_up_v(olat_h, b16(wuv.transpose(0, 2, 1)))
        o_dsa = b16(odsa.reshape(DB, DSA_KV_HEADS, DSA_GROUP, TP, LANES)[:, :, :, :TS]
                    .transpose(0, 3, 1, 2, 4).reshape(DB * TS, DSA_HEADS * LANES))
        mq_s = jnp.pad(mq.reshape(DB, TS, -1), ((0, 0), (0, TP - TS), (0, 0)))
        mem_b16 = lambda c: b16(c[l]).reshape(DB, c.shape[2], -1)
        o_mem = _mem_attend(mq_s, mem_b16(cache_mem_k), mem_b16(cache_mem_v))[:, :TS].reshape(DB * TS, -1)
        xs = _merge(xs, o_mla, o_dsa, o_mem, c, *merge_w)
        xs = _ffn_half(xs, *ffn2)
        st["ckv_s"].append(ckv.reshape(DB, TS, MLA_D_CKV))
        st["kr_s"].append(kr.reshape(DB, TS, MLA_D_ROPE))
        st["dk_s"].append(dk.reshape(DB, TS, DSA_KV_HEADS, DSA_HEAD_DIM))
        st["dv_s"].append(dv.reshape(DB, TS, DSA_KV_HEADS, DSA_HEAD_DIM))
        st["ik_s"].append(ik.reshape(DB, TS, IDX_DIM))

    return (xp.reshape(B, S, D), xs.reshape(DB, TS, D),
            jnp.stack(st["ckv_p"]), jnp.stack(st["kr_p"]), jnp.stack(st["dk_p"]), jnp.stack(st["dv_p"]),
            jnp.stack(st["ik_p"]), jnp.stack(st["mk_p"]), jnp.stack(st["mv_p"]),
            jnp.stack(st["ckv_s"]), jnp.stack(st["kr_s"]), jnp.stack(st["dk_s"]), jnp.stack(st["dv_s"]),
            jnp.stack(st["ik_s"]))
```

```python
import functools

import jax
import jax.numpy as jnp
from jax import lax
from jax.experimental import pallas as pl
from jax.experimental.pallas import tpu as pltpu

F32, BF16, I32 = jnp.float32, jnp.bfloat16, jnp.int32

D_MODEL = 2048
D_FF = 5632
PAST_LEN = 8192
PAGE_SIZE = 128
ROPE_THETA = 500000.0
EPS = 1e-6
MLA_HEADS, MLA_D_NOPE, MLA_D_ROPE, MLA_D_V, MLA_D_CQ, MLA_D_CKV = 8, 128, 64, 128, 768, 512
MLA_D_QK = MLA_D_NOPE + MLA_D_ROPE
DSA_HEADS, DSA_KV_HEADS, DSA_HEAD_DIM = 8, 2, 128
DSA_GROUP = DSA_HEADS // DSA_KV_HEADS
DSA_ROT = DSA_HEAD_DIM // 4
IDX_HEADS, IDX_DIM = 16, 64
IDX_ROT = IDX_DIM // 4
TOPK_MAX = 256
MEM_HEADS, MEM_HEAD_DIM = 4, 256
N_BRANCH = 3

LANES = 128
SUBLANES = 8
NEG = -1e30
INT_MIN = -2 ** 31
VMEM_LIMIT = 56 * 1024 * 1024

MLA_SCALE = MLA_D_QK ** -0.5
DSA_SCALE = DSA_HEAD_DIM ** -0.5
MEM_SCALE = MEM_HEAD_DIM ** -0.5

C_CQ = 0
C_CKV = C_CQ + MLA_D_CQ
C_DQ = C_CKV + MLA_D_CKV
C_DK = C_DQ + DSA_HEADS * DSA_HEAD_DIM
C_DV = C_DK + DSA_KV_HEADS * DSA_HEAD_DIM
C_IQ = C_DV + DSA_KV_HEADS * DSA_HEAD_DIM
C_MQ = C_IQ + IDX_HEADS * IDX_DIM
C_KRI = C_MQ + MEM_HEADS * MEM_HEAD_DIM
C_IW = C_KRI + LANES
C_GATE = C_IW + LANES
RT_MLA, RT_DSA, RT_IDX, RT_KRI, RT_SLOTS = 0, 3, 6, 9, 14

NT_DIMS = (((1,), (1,)), ((), ()))
FLASH_HEAD_GROUP = 4
SAMPLE_PAGES_PER_STEP = 16
SAMPLE_CHUNK_PAGES = 4


def _cparams(*sem):
    return pltpu.CompilerParams(dimension_semantics=sem, vmem_limit_bytes=VMEM_LIMIT)


def _rms(x, g):
    return x * lax.rsqrt(jnp.mean(x * x, axis=-1, keepdims=True) + EPS) * g


def _dot(a, b):
    return jnp.dot(a, b, preferred_element_type=F32)


def _dot_nt(a, b):
    return lax.dot_general(a, b, NT_DIMS, preferred_element_type=F32)


def _roll(x, s):
    return pltpu.roll(x, s % x.shape[-1], axis=x.ndim - 1)


def _lane_tile(x, width):
    n = width // LANES
    return x if n == 1 else jnp.concatenate([x] * n, axis=1)


def _softmax_init(m_ref, l_ref, acc_ref):
    m_ref[...] = jnp.full(m_ref.shape, NEG, F32)
    l_ref[...] = jnp.zeros(l_ref.shape, F32)
    acc_ref[...] = jnp.zeros(acc_ref.shape, F32)


def _softmax_weights(s, m_ref, l_ref):
    m_prev = m_ref[...]
    m_new = jnp.maximum(m_prev, jnp.max(s, axis=1, keepdims=True))
    alpha = jnp.exp(m_prev - m_new)
    p = jnp.exp(s - _lane_tile(m_new, s.shape[1]))
    l_ref[...] = alpha * l_ref[...] + jnp.sum(p, axis=1, keepdims=True)
    m_ref[...] = m_new
    return p.astype(BF16), alpha


def _softmax_accumulate(p, alpha, v, acc_ref):
    acc_ref[...] = _lane_tile(alpha, acc_ref.shape[-1]) * acc_ref[...] + _dot(p, v)


def _softmax_updates(scores, values, states):
    weights = [_softmax_weights(s, m, l) for s, (m, l, _) in zip(scores, states)]
    for (p, alpha), v, (_, _, acc) in zip(weights, values, states):
        _softmax_accumulate(p, alpha, v, acc)


def _ordinal_to_float(u):
    k = u ^ INT_MIN
    return pltpu.bitcast(k ^ ((k >> 31) & 0x7FFFFFFF), F32)


def _ffn_kernel(x_ref, g_ref, wg_ref, wu_ref, wd_ref, o_ref, h_ref):
    @pl.when(pl.program_id(1) == 0)
    def _():
        x = x_ref[...]
        h_ref[...] = _rms(x, g_ref[...]).astype(BF16)
        o_ref[...] = x

    h = h_ref[...]
    a = _dot(h, wg_ref[...])
    b = _dot(h, wu_ref[...])
    act = (a * jax.nn.sigmoid(a) * b).astype(BF16)
    o_ref[...] += 0.5 * _dot(act, wd_ref[...])


def _ffn_half(x, g, wg, wu, wd):
    T, D = x.shape
    F = wg.shape[1]
    tm = min(T, 512)
    tf = min(F, 512)
    return pl.pallas_call(
        _ffn_kernel,
        grid=(T // tm, F // tf),
        in_specs=[
            pl.BlockSpec((tm, D), lambda i, j: (i, 0)),
            pl.BlockSpec((1, D), lambda i, j: (0, 0)),
            pl.BlockSpec((D, tf), lambda i, j: (0, j)),
            pl.BlockSpec((D, tf), lambda i, j: (0, j)),
            pl.BlockSpec((tf, D), lambda i, j: (j, 0)),
        ],
        out_specs=pl.BlockSpec((tm, D), lambda i, j: (i, 0)),
        out_shape=jax.ShapeDtypeStruct((T, D), F32),
        scratch_shapes=[pltpu.VMEM((tm, D), BF16)],
        compiler_params=_cparams("parallel", "arbitrary"),
    )(x, g, wg, wu, wd)


def _proj_kernel(x_ref, g_ref, w_ref, o_ref, h_ref):
    @pl.when(pl.program_id(1) == 0)
    def _():
        h_ref[...] = _rms(x_ref[...], g_ref[...]).astype(BF16)

    o_ref[...] = _dot_nt(h_ref[...], w_ref[...])


def _project(x, g, w_t):
    T, D = x.shape
    N = w_t.shape[0]
    tm = min(T, 1024)
    tn = 512
    return pl.pallas_call(
        _proj_kernel,
        grid=(T // tm, N // tn),
        in_specs=[
            pl.BlockSpec((tm, D), lambda i, j: (i, 0)),
            pl.BlockSpec((1, D), lambda i, j: (0, 0)),
            pl.BlockSpec((tn, D), lambda i, j: (j, 0)),
        ],
        out_specs=pl.BlockSpec((tm, tn), lambda i, j: (i, j)),
        out_shape=jax.ShapeDtypeStruct((T, N), F32),
        scratch_shapes=[pltpu.VMEM((tm, D), BF16)],
        compiler_params=_cparams("parallel", "arbitrary"),
    )(x, g, w_t)


def _rope3(x, rt_ref, slot, half):
    c = rt_ref[:, slot * LANES:(slot + 1) * LANES]
    s1 = rt_ref[:, (slot + 1) * LANES:(slot + 2) * LANES]
    s2 = rt_ref[:, (slot + 2) * LANES:(slot + 3) * LANES]
    return x * c + _roll(x, -half) * s1 + _roll(x, half) * s2


def _epilogue_kernel(c_ref, rt_ref, gcq_ref, gqn_ref, gqr_ref, gckv_ref, gkr_ref, gdq_ref, gdk_ref, gmq_ref,
                     wuq_ref, wx_ref, *outs, sample):
    if sample:
        ckv_o, kr_o, dk_o, dv_o, ik_o, iw_o, qlat_o, qr_o, dq_o, iq_o, mq_o = outs
    else:
        (ckv_o, kr_o, dk_o, dv_o, ik_o, iw_o, qcat_o, kcat_o, vb_o, dq_o, dkb_o, dvb_o, iq_o, ikb_o, mq_o) = outs
    tm = c_ref.shape[0]
    lo = lax.broadcasted_iota(I32, (tm, LANES), 1) < MLA_D_ROPE

    cqn = _rms(c_ref[:, C_CQ:C_CQ + MLA_D_CQ], gcq_ref[...]).astype(BF16)
    q = _dot(cqn, wuq_ref[...])
    rope0 = MLA_HEADS * MLA_D_NOPE
    for p in range(MLA_HEADS // 2):
        xr = q[:, rope0 + p * LANES: rope0 + (p + 1) * LANES]
        x2 = xr * xr
        ss_rope = (jnp.sum(jnp.where(lo, x2, 0.0), axis=1, keepdims=True),
                   jnp.sum(jnp.where(lo, 0.0, x2), axis=1, keepdims=True))
        inv = []
        for k in range(2):
            h = 2 * p + k
            qn = q[:, h * MLA_D_NOPE:(h + 1) * MLA_D_NOPE]
            ss = jnp.sum(qn * qn, axis=1, keepdims=True) + ss_rope[k]
            inv.append(lax.rsqrt(ss * (1.0 / MLA_D_QK) + EPS))
            qn = (qn * inv[k] * gqn_ref[...] * MLA_SCALE).astype(BF16)
            if sample:
                qlat_o[:, h * MLA_D_CKV:(h + 1) * MLA_D_CKV] = _dot(qn, wx_ref[h]).astype(BF16)
            else:
                qcat_o[:, h * 2 * LANES: h * 2 * LANES + LANES] = qn
        xr = xr * jnp.where(lo, inv[0], inv[1]) * gqr_ref[...] * MLA_SCALE
        xr = _rope3(xr, rt_ref, RT_MLA, MLA_D_ROPE // 2)
        if sample:
            qr_o[:, p * LANES:(p + 1) * LANES] = xr.astype(BF16)
        else:
            qcat_o[:, (4 * p + 1) * LANES:(4 * p + 2) * LANES] = jnp.where(lo, xr, 0.0).astype(BF16)
            qcat_o[:, (4 * p + 3) * LANES:(4 * p + 4) * LANES] = jnp.where(lo, _roll(xr, MLA_D_ROPE), 0.0).astype(BF16)

    ckv = _rms(c_ref[:, C_CKV:C_CKV + MLA_D_CKV], gckv_ref[...])
    ckv_o[...] = ckv
    x = c_ref[:, C_KRI:C_KRI + LANES]
    ssk = jnp.sum(jnp.where(lo, x * x, 0.0), axis=1, keepdims=True)
    x = jnp.where(lo, x * lax.rsqrt(ssk * (1.0 / MLA_D_ROPE) + EPS) * gkr_ref[...], x)
    s = RT_KRI
    kri = (x * rt_ref[:, s * LANES:(s + 1) * LANES]
           + _roll(x, -(MLA_D_ROPE // 2)) * rt_ref[:, (s + 1) * LANES:(s + 2) * LANES]
           + _roll(x, MLA_D_ROPE // 2) * rt_ref[:, (s + 2) * LANES:(s + 3) * LANES]
           + _roll(x, -(IDX_ROT // 2)) * rt_ref[:, (s + 3) * LANES:(s + 4) * LANES]
           + _roll(x, IDX_ROT // 2) * rt_ref[:, (s + 4) * LANES:(s + 5) * LANES])
    kr_o[...] = kri[:, :MLA_D_ROPE]
    ik_o[...] = kri[:, MLA_D_ROPE:]
    if not sample:
        ikb_o[...] = kri[:, MLA_D_ROPE:].astype(BF16)
        kv = _dot(ckv.astype(BF16), wx_ref[...])
        vb_o[...] = kv[:, MLA_HEADS * MLA_D_NOPE:].astype(BF16)
        krb = jnp.where(lo, kri, 0.0).astype(BF16)
        for h in range(MLA_HEADS):
            kcat_o[:, h * 2 * LANES: h * 2 * LANES + LANES] = kv[:, h * MLA_D_NOPE:(h + 1) * MLA_D_NOPE].astype(BF16)
            kcat_o[:, h * 2 * LANES + LANES:(h + 1) * 2 * LANES] = krb

    for h in range(DSA_HEADS):
        x = _rms(c_ref[:, C_DQ + h * LANES: C_DQ + (h + 1) * LANES], gdq_ref[...])
        dq_o[:, h * LANES:(h + 1) * LANES] = (_rope3(x, rt_ref, RT_DSA, DSA_ROT // 2) * DSA_SCALE).astype(BF16)
    for g in range(DSA_KV_HEADS):
        x = _rms(c_ref[:, C_DK + g * LANES: C_DK + (g + 1) * LANES], gdk_ref[...])
        x = _rope3(x, rt_ref, RT_DSA, DSA_ROT // 2)
        dk_o[:, g * LANES:(g + 1) * LANES] = x
        if not sample:
            dkb_o[:, g * LANES:(g + 1) * LANES] = x.astype(BF16)
    x = c_ref[:, C_DV:C_DV + DSA_KV_HEADS * DSA_HEAD_DIM]
    dv_o[...] = x
    if not sample:
        dvb_o[...] = x.astype(BF16)

    for p in range(IDX_HEADS // 2):
        x = c_ref[:, C_IQ + p * LANES: C_IQ + (p + 1) * LANES]
        iq_o[:, p * LANES:(p + 1) * LANES] = _rope3(x, rt_ref, RT_IDX, IDX_ROT // 2).astype(BF16)
    iw_o[...] = c_ref[:, C_IW:C_IW + LANES]

    for h in range(MEM_HEADS):
        x = _rms(c_ref[:, C_MQ + h * MEM_HEAD_DIM: C_MQ + (h + 1) * MEM_HEAD_DIM], gmq_ref[...])
        mq_o[:, h * MEM_HEAD_DIM:(h + 1) * MEM_HEAD_DIM] = (x * MEM_SCALE).astype(BF16)


def _epilogue(c, rt, gains, wuq, wx, sample):
    T = c.shape[0]
    tm = min(T, 256)
    n_rt = rt.shape[0] // tm
    row = lambda n: pl.BlockSpec((tm, n), lambda i: (i, 0))
    full = lambda a: pl.BlockSpec(a.shape, lambda i: (0,) * a.ndim)
    widths_f32 = [MLA_D_CKV, MLA_D_ROPE, 2 * LANES, 2 * LANES, IDX_DIM, LANES]
    if sample:
        widths_b16 = [MLA_HEADS * MLA_D_CKV, MLA_HEADS * MLA_D_ROPE, 8 * LANES, 8 * LANES, 8 * LANES]
    else:
        widths_b16 = [16 * LANES, 16 * LANES, 8 * LANES, 8 * LANES, 2 * LANES, 2 * LANES, 8 * LANES, IDX_DIM, 8 * LANES]
    out_shape = ([jax.ShapeDtypeStruct((T, n), F32) for n in widths_f32]
                 + [jax.ShapeDtypeStruct((T, n), BF16) for n in widths_b16])
    return pl.pallas_call(
        functools.partial(_epilogue_kernel, sample=sample),
        grid=(T // tm,),
        in_specs=[pl.BlockSpec((tm, C_GATE), lambda i: (i, 0)),
                  pl.BlockSpec((tm, RT_SLOTS * LANES), lambda i: (i % n_rt, 0))]
                 + [full(g) for g in gains] + [full(wuq), full(wx)],
        out_specs=[row(n) for n in widths_f32 + widths_b16],
        out_shape=out_shape,
        compiler_params=_cparams("parallel"),
    )(c, rt, *gains, wuq, wx)


def _mla_flash_kernel(q_ref, k_ref, v_ref, o_ref, m_ref, l_ref, acc_ref):
    i, j = pl.program_id(1), pl.program_id(2)
    tq, tk = q_ref.shape[0], k_ref.shape[0]
    qk, dv = 2 * LANES, MLA_D_V

    @pl.when(j == 0)
    def _():
        _softmax_init(m_ref, l_ref, acc_ref)

    def step(mask):
        for h0 in range(0, MLA_HEADS, FLASH_HEAD_GROUP):
            heads = range(h0, h0 + FLASH_HEAD_GROUP)
            scores = [_dot_nt(q_ref[:, h * qk:(h + 1) * qk], k_ref[:, h * qk:(h + 1) * qk]) for h in heads]
            if mask is not None:
                scores = [jnp.where(mask, s, NEG) for s in scores]
            _softmax_updates(scores, [v_ref[:, h * dv:(h + 1) * dv] for h in heads],
                             [(m_ref.at[h], l_ref.at[h], acc_ref.at[h]) for h in heads])

    @pl.when(j < i)
    def _():
        step(None)

    @pl.when(j == i)
    def _():
        step(lax.broadcasted_iota(I32, (tq, tk), 1) <= lax.broadcasted_iota(I32, (tq, tk), 0))
        for h in range(MLA_HEADS):
            o_ref[:, h * dv:(h + 1) * dv] = (acc_ref[h] / l_ref[h]).astype(o_ref.dtype)


def _mla_prompt(qcat, kcat, vb, B, T):
    tq = min(T, 512)
    nq = T // tq
    H = MLA_HEADS
    return pl.pallas_call(
        _mla_flash_kernel,
        grid=(B, nq, nq),
        in_specs=[
            pl.BlockSpec((tq, H * 2 * LANES), lambda b, i, j: (b * nq + i, 0)),
            pl.BlockSpec((tq, H * 2 * LANES), lambda b, i, j: (b * nq + jnp.minimum(i, j), 0)),
            pl.BlockSpec((tq, H * MLA_D_V), lambda b, i, j: (b * nq + jnp.minimum(i, j), 0)),
        ],
        out_specs=pl.BlockSpec((tq, H * MLA_D_V), lambda b, i, j: (b * nq + i, 0)),
        out_shape=jax.ShapeDtypeStruct((B * T, H * MLA_D_V), BF16),
        scratch_shapes=[pltpu.VMEM((H, tq, LANES), F32), pltpu.VMEM((H, tq, LANES), F32),
                        pltpu.VMEM((H, tq, MLA_D_V), F32)],
        compiler_params=_cparams("parallel", "parallel", "arbitrary"),
    )(qcat, kcat, vb)


def _select_threshold(count_fn, shape, ksel, n_index_bits):
    def value_step(s, prefix):
        cand = prefix | lax.shift_left(jnp.int32(1), 31 - s)
        cand_f = _ordinal_to_float(cand)
        n = count_fn(lambda x, col: x >= cand_f)
        return jnp.where(n >= ksel, cand, prefix)

    thr = _ordinal_to_float(lax.fori_loop(0, 32, value_step, jnp.zeros(shape, I32)))
    all_visible = count_fn(lambda x, col: x > -jnp.inf) <= ksel
    need = ksel - count_fn(lambda x, col: x > thr)
    n_ge = count_fn(lambda x, col: x >= thr)

    def search_cut():
        def cut_step(s, cut):
            cand = cut | lax.shift_left(jnp.int32(1), n_index_bits - 1 - s)
            n = count_fn(lambda x, col: (x == thr) & (col < cand))
            return jnp.where(n <= need, cand, cut)
        return lax.fori_loop(0, n_index_bits, cut_step, jnp.zeros(shape, I32))

    cut = lax.cond(jnp.max(n_ge) > ksel, search_cut, lambda: jnp.full(shape, 2 ** n_index_bits, I32))
    return all_visible, thr, cut


def _dsa_prompt_kernel(iq_ref, iw_ref, ik_ref, dq_ref, dk_ref, dv_ref, o_ref,
                       key_ref, bias_ref, m_ref, l_ref, acc_ref, *, ck, ksel):
    i = pl.program_id(1)
    tq = dq_ref.shape[0]
    T = ik_ref.shape[0]
    nch = ((i + 1) * tq + ck - 1) // ck
    row = i * tq + lax.broadcasted_iota(I32, (tq, ck), 0)
    col0 = lax.broadcasted_iota(I32, (tq, ck), 1)

    def idx_chunk(c, carry):
        ikc = ik_ref[pl.ds(pl.multiple_of(c * ck, ck), ck), :]
        acc = jnp.zeros((tq, ck), F32)
        for h in range(IDX_HEADS):
            acc = acc + jnp.maximum(_dot_nt(iq_ref[h], ikc), 0.0) * iw_ref[:, h:h + 1]
        key_ref[c] = jnp.where(c * ck + col0 <= row, acc, -jnp.inf)
        return carry

    lax.fori_loop(0, nch, idx_chunk, 0)

    def count_fn(pred):
        def body(c, a):
            hit = jnp.where(pred(key_ref[c], c * ck + col0), 1.0, 0.0)
            for t in range(ck // LANES):
                a = a + hit[:, t * LANES:(t + 1) * LANES]
            return a
        a = lax.fori_loop(0, nch, body, jnp.zeros((tq, LANES), F32))
        return jnp.sum(a, axis=1, keepdims=True)

    all_visible, thr, cut = _select_threshold(count_fn, (tq, 1), ksel, T.bit_length())

    def bias_chunk(c, carry):
        x = key_ref[c]
        col = c * ck + col0
        sel = (col <= row) & (all_visible | (x > thr) | ((x == thr) & (col < cut)))
        bias_ref[c] = jnp.where(sel, 0.0, NEG)
        return carry

    lax.fori_loop(0, nch, bias_chunk, 0)

    qg = [jnp.concatenate([dq_ref[:, (g * DSA_GROUP + r) * LANES:(g * DSA_GROUP + r + 1) * LANES]
                           for r in range(DSA_GROUP)], axis=0) for g in range(DSA_KV_HEADS)]
    _softmax_init(m_ref, l_ref, acc_ref)

    def attn_chunk(c, carry):
        k0 = pl.multiple_of(c * ck, ck)
        bias = jnp.concatenate([bias_ref[c]] * DSA_GROUP, axis=0)
        heads = range(DSA_KV_HEADS)
        scores = [_dot_nt(qg[g], dk_ref[pl.ds(k0, ck), g * LANES:(g + 1) * LANES]) + bias for g in heads]
        _softmax_updates(scores, [dv_ref[pl.ds(k0, ck), g * LANES:(g + 1) * LANES] for g in heads],
                         [(m_ref.at[g], l_ref.at[g], acc_ref.at[g]) for g in heads])
        return carry

    lax.fori_loop(0, nch, attn_chunk, 0)
    for g in range(DSA_KV_HEADS):
        o = acc_ref[g] / l_ref[g]
        for r in range(DSA_GROUP):
            o_ref[:, (g * DSA_GROUP + r) * LANES:(g * DSA_GROUP + r + 1) * LANES] = (
                o[r * tq:(r + 1) * tq].astype(o_ref.dtype))


def _dsa_prompt(iq3, iw, ikb, dq, dkb, dvb, B, T):
    tq = min(T, 256)
    ck = min(T, 512)
    nq = T // tq
    ksel = min(TOPK_MAX, T // 4)
    return pl.pallas_call(
        functools.partial(_dsa_prompt_kernel, ck=ck, ksel=ksel),
        grid=(B, nq),
        in_specs=[
            pl.BlockSpec((IDX_HEADS, tq, IDX_DIM), lambda b, i: (0, b * nq + i, 0)),
            pl.BlockSpec((tq, LANES), lambda b, i: (b * nq + i, 0)),
            pl.BlockSpec((T, IDX_DIM), lambda b, i: (b, 0)),
            pl.BlockSpec((tq, DSA_HEADS * LANES), lambda b, i: (b * nq + i, 0)),
            pl.BlockSpec((T, DSA_KV_HEADS * LANES), lambda b, i: (b, 0)),
            pl.BlockSpec((T, DSA_KV_HEADS * LANES), lambda b, i: (b, 0)),
        ],
        out_specs=pl.BlockSpec((tq, DSA_HEADS * LANES), lambda b, i: (b * nq + i, 0)),
        out_shape=jax.ShapeDtypeStruct((B * T, DSA_HEADS * LANES), BF16),
        scratch_shapes=[pltpu.VMEM((T // ck, tq, ck), F32), pltpu.VMEM((T // ck, tq, ck), F32)]
                       + [pltpu.VMEM((DSA_KV_HEADS, DSA_GROUP * tq, LANES), F32)] * 3,
        compiler_params=_cparams("parallel", "arbitrary"),
    )(iq3, iw, ikb, dq, dkb, dvb)


def _memkv_kernel(x_ref, g_ref, w_ref, gk_ref, k_ref, v_ref):
    h = _rms(x_ref[...], g_ref[...]).astype(BF16)
    kv = _dot(h, w_ref[...])
    n = MEM_HEADS * MEM_HEAD_DIM
    for hd in range(MEM_HEADS):
        k_ref[:, hd * MEM_HEAD_DIM:(hd + 1) * MEM_HEAD_DIM] = _rms(
            kv[:, hd * MEM_HEAD_DIM:(hd + 1) * MEM_HEAD_DIM], gk_ref[...])
    v_ref[...] = kv[:, n:]


def _memory_kv(mem, g_in, w, g_k):
    T, D = mem.shape
    n = MEM_HEADS * MEM_HEAD_DIM
    tm = min(T, 256)
    return pl.pallas_call(
        _memkv_kernel,
        grid=(T // tm,),
        in_specs=[pl.BlockSpec((tm, D), lambda i: (i, 0)), pl.BlockSpec((1, D), lambda i: (0, 0)),
                  pl.BlockSpec((D, 2 * n), lambda i: (0, 0)), pl.BlockSpec((1, MEM_HEAD_DIM), lambda i: (0, 0))],
        out_specs=[pl.BlockSpec((tm, n), lambda i: (i, 0))] * 2,
        out_shape=[jax.ShapeDtypeStruct((T, n), F32)] * 2,
        compiler_params=_cparams("parallel"),
    )(mem, g_in, w, g_k)


def _mem_attend_kernel(q_ref, k_ref, v_ref, o_ref):
    heads = [slice(h * MEM_HEAD_DIM, (h + 1) * MEM_HEAD_DIM) for h in range(MEM_HEADS)]
    scores = [_dot_nt(q_ref[:, sl], k_ref[:, sl]) for sl in heads]
    probs = []
    for s in scores:
        p = jnp.exp(s - jnp.max(s, axis=1, keepdims=True))
        probs.append((p / jnp.sum(p, axis=1, keepdims=True)).astype(BF16))
    for p, sl in zip(probs, heads):
        o_ref[:, sl] = _dot(p, v_ref[:, sl]).astype(o_ref.dtype)


def _mem_attend(mq, mk, mv):
    G, R, n = mq.shape
    M = mk.shape[1]
    tm = min(R, 512)
    mem = pl.BlockSpec((None, M, n), lambda g, i: (g, 0, 0))
    return pl.pallas_call(
        _mem_attend_kernel,
        grid=(G, R // tm),
        in_specs=[pl.BlockSpec((None, tm, n), lambda g, i: (g, i, 0)), mem, mem],
        out_specs=pl.BlockSpec((None, tm, n), lambda g, i: (g, i, 0)),
        out_shape=jax.ShapeDtypeStruct((G, R, n), BF16),
        compiler_params=_cparams("parallel", "parallel"),
    )(mq, mk, mv)


def _merge_kernel(x_ref, oa_ref, ob_ref, oc_ref, ga_ref, gb_ref, gc_ref, wa_ref, wb_ref, wc_ref, wo_ref, o_ref):
    @pl.when(pl.program_id(1) == 0)
    def _():
        o_ref[...] = x_ref[...]

    merged = (jax.nn.sigmoid(ga_ref[...]) * _dot(oa_ref[...], wa_ref[...])
              + jax.nn.sigmoid(gb_ref[...]) * _dot(ob_ref[...], wb_ref[...])
              + jax.nn.sigmoid(gc_ref[...]) * _dot(oc_ref[...], wc_ref[...]))
    o_ref[...] += _dot(merged.astype(BF16), wo_ref[...])


def _merge(x, oa, ob, oc, c, wa, wb, wc, wo):
    T, D = x.shape
    n = oa.shape[1]
    tm = min(T, 512)
    tn = min(D, 512)
    g0 = C_GATE // tn
    nb = D // tn
    gate = lambda k: pl.BlockSpec((tm, tn), lambda i, j: (i, g0 + k * nb + j))
    branch_w = pl.BlockSpec((n, tn), lambda i, j: (0, j))
    rows = pl.BlockSpec((tm, n), lambda i, j: (i, 0))
    return pl.pallas_call(
        _merge_kernel,
        grid=(T // tm, nb),
        in_specs=[pl.BlockSpec((tm, D), lambda i, j: (i, 0)), rows, rows, rows, gate(0), gate(1), gate(2),
                  branch_w, branch_w, branch_w, pl.BlockSpec((tn, D), lambda i, j: (j, 0))],
        out_specs=pl.BlockSpec((tm, D), lambda i, j: (i, 0)),
        out_shape=jax.ShapeDtypeStruct((T, D), F32),
        compiler_params=_cparams("parallel", "arbitrary"),
    )(x, oa, ob, oc, c, c, c, wa, wb, wc, wo)


def _sample_scores_kernel(pt_ref, iq_ref, iw_ref, ikn_ref, ikt_hbm, o_ref, buf, sem, *, n_pages, group):
    b = pl.program_id(0)
    slot = b % 2
    R = iq_ref.shape[0]
    TS = R // IDX_HEADS
    iq = iq_ref[...]
    iw = iw_ref[...]

    def page_copies(seq, slot):
        return [pltpu.make_async_copy(ikt_hbm.at[pt_ref[seq * n_pages + k]], buf.at[slot, k], sem.at[slot])
                for k in range(n_pages)]

    @pl.when(b == 0)
    def _():
        for c in page_copies(0, 0):
            c.start()

    @pl.when(b + 1 < pl.num_programs(0))
    def _():
        for c in page_copies(b + 1, 1 - slot):
            c.start()

    for c in page_copies(b, slot):
        c.wait()

    def head_sum(s):
        s = jnp.maximum(s, 0.0) * iw
        return jnp.sum(s.reshape(TS, IDX_HEADS, s.shape[1]), axis=1)

    for g0 in range(0, n_pages, group):
        kt = jnp.concatenate([buf[slot, k].astype(BF16) for k in range(g0, g0 + group)], axis=1)
        o_ref[:, g0 * PAGE_SIZE:(g0 + group) * PAGE_SIZE] = head_sum(_dot(iq, kt))
    o_ref[:, n_pages * PAGE_SIZE:] = head_sum(_dot_nt(iq, ikn_ref[...].astype(BF16)))


def _sample_scores(pt, iq, iw, ik_new, cache_ikt, n_pages):
    DB, R, _ = iq.shape
    TS = R // IDX_HEADS
    W = (n_pages + 1) * PAGE_SIZE
    grid_spec = pltpu.PrefetchScalarGridSpec(
        num_scalar_prefetch=1,
        grid=(DB,),
        in_specs=[pl.BlockSpec((None, R, IDX_DIM), lambda b, pt: (b, 0, 0)),
                  pl.BlockSpec((None, R, 1), lambda b, pt: (b, 0, 0)),
                  pl.BlockSpec((None, PAGE_SIZE, IDX_DIM), lambda b, pt: (b, 0, 0)),
                  pl.BlockSpec(memory_space=pl.ANY)],
        out_specs=pl.BlockSpec((None, TS, W), lambda b, pt: (b, 0, 0)),
        scratch_shapes=[pltpu.VMEM((2, n_pages, IDX_DIM, PAGE_SIZE), cache_ikt.dtype),
                        pltpu.SemaphoreType.DMA((2,))],
    )
    return pl.pallas_call(
        functools.partial(_sample_scores_kernel, n_pages=n_pages, group=min(n_pages, 8)),
        grid_spec=grid_spec,
        out_shape=jax.ShapeDtypeStruct((DB, TS, W), F32),
        compiler_params=_cparams("arbitrary"),
    )(pt, iq, iw, ik_new, cache_ikt)


def _sample_select_kernel(s_ref, o_ref, *, past, ksel, tp):
    RB, W = s_ref.shape
    t = lax.broadcasted_iota(I32, (RB, W), 0) % tp
    col = lax.broadcasted_iota(I32, (RB, W), 1)
    visible = col <= past + t
    x = jnp.where(visible, s_ref[...], -jnp.inf)

    def count_fn(pred):
        hit = jnp.where(pred(x, col), 1.0, 0.0)
        a = hit[:, :LANES]
        for c in range(1, W // LANES):
            a = a + hit[:, c * LANES:(c + 1) * LANES]
        return jnp.sum(a, axis=1, keepdims=True)

    all_visible, thr, cut = _select_threshold(count_fn, (RB, 1), ksel, W.bit_length())
    sel = visible & (all_visible | (x > thr) | ((x == thr) & (col < cut)))
    o_ref[...] = jnp.where(sel, 1.0, 0.0).astype(o_ref.dtype)


def _sample_select(scores, past, ksel, tp):
    R, W = scores.shape
    rb = min(R, 64)
    return pl.pallas_call(
        functools.partial(_sample_select_kernel, past=past, ksel=ksel, tp=tp),
        grid=(R // rb,),
        in_specs=[pl.BlockSpec((rb, W), lambda i: (i, 0))],
        out_specs=pl.BlockSpec((rb, W), lambda i: (i, 0)),
        out_shape=jax.ShapeDtypeStruct((R, W), BF16),
        compiler_params=_cparams("parallel"),
    )(scores)


def _sample_attend_kernel(pt_ref, qlat_ref, qr_ref, dq_ref, sel_ref, seln_ref,
                          ckvn_ref, krn_ref, dkn_ref, dvn_ref, ckv_hbm, krt_hbm, dk_hbm, dv_hbm,
                          olat_ref, odsa_ref, ma_ref, la_ref, acca_ref, md_ref, ld_ref, accd_ref,
                          ckv_cat, krt_cat, dk_cat, dv_cat, ckv_buf, krt_buf, dk_buf, dv_buf, sem, *, pages, n_steps):
    b, j = pl.program_id(0), pl.program_id(1)
    step = b * n_steps + j
    n_total = pl.num_programs(0) * n_steps
    slot = step % 2
    RA, RD = qlat_ref.shape[0], dq_ref.shape[0]
    states = [(ma_ref, la_ref, acca_ref), (md_ref, ld_ref, accd_ref)]

    def page_copies(step, slot):
        copies = []
        for k in range(pages):
            page = pt_ref[step * pages + k]
            copies += [pltpu.make_async_copy(hbm.at[page], buf.at[slot, k], sem.at[slot, a])
                       for a, (hbm, buf) in enumerate(((ckv_hbm, ckv_buf), (krt_hbm, krt_buf),
                                                       (dk_hbm, dk_buf), (dv_hbm, dv_buf)))]
        return copies

    @pl.when(step == 0)
    def _():
        for c in page_copies(0, 0):
            c.start()

    @pl.when(step + 1 < n_total)
    def _():
        for c in page_copies(step + 1, 1 - slot):
            c.start()

    for c in page_copies(step, slot):
        c.wait()
    ckv_pages, krt_pages, dk_pages, dv_pages = ([buf.at[slot, k] for k in range(pages)]
                                                for buf in (ckv_buf, krt_buf, dk_buf, dv_buf))

    @pl.when(j == 0)
    def _():
        _softmax_init(ma_ref, la_ref, acca_ref)
        _softmax_init(md_ref, ld_ref, accd_ref)

    qlat, qr, dq = qlat_ref[...], qr_ref[...], dq_ref[...]

    def dsa_mask(flags_ref, s):
        flags = flags_ref[...].astype(F32)
        selected = jnp.concatenate([flags] * (RD // flags.shape[0]), axis=0) > 0.5
        head_of_col = lax.broadcasted_iota(I32, selected.shape, 1) % DSA_KV_HEADS
        head_of_row = lax.broadcasted_iota(I32, selected.shape, 0) // (RD // DSA_KV_HEADS)
        return jnp.where(selected & (head_of_col == head_of_row), s, NEG)

    s_mla, s_dsa = [], []
    for c0 in range(0, pages, SAMPLE_CHUNK_PAGES):
        for k in range(c0, min(c0 + SAMPLE_CHUNK_PAGES, pages)):
            ckv_cat[k * PAGE_SIZE:(k + 1) * PAGE_SIZE, :] = ckv_pages[k][...].astype(BF16)
            krt_cat[:, k * PAGE_SIZE:(k + 1) * PAGE_SIZE] = krt_pages[k][...].astype(BF16)
            dk_cat[k * 2 * PAGE_SIZE:(k + 1) * 2 * PAGE_SIZE, :] = dk_pages[k][...].astype(BF16)
            dv_cat[k * 2 * PAGE_SIZE:(k + 1) * 2 * PAGE_SIZE, :] = dv_pages[k][...].astype(BF16)
        keys = slice(c0 * PAGE_SIZE, min(c0 + SAMPLE_CHUNK_PAGES, pages) * PAGE_SIZE)
        rows = slice(2 * keys.start, 2 * keys.stop)
        s_mla.append(_dot_nt(qlat, ckv_cat[keys, :]) + _dot(qr, krt_cat[:, keys]))
        s_dsa.append(_dot_nt(dq, dk_cat[rows, :]))
    _softmax_updates([jnp.concatenate(s_mla, axis=1), dsa_mask(sel_ref, jnp.concatenate(s_dsa, axis=1))],
                     [ckv_cat[...], dv_cat[...]], states)

    @pl.when(j == n_steps - 1)
    def _():
        t = lax.broadcasted_iota(I32, (RA, PAGE_SIZE), 0) // MLA_HEADS
        cn = lax.broadcasted_iota(I32, (RA, PAGE_SIZE), 1)
        ckvn = ckvn_ref[...].astype(BF16)
        s = _dot_nt(qlat, ckvn) + _dot_nt(qr, krn_ref[...].astype(BF16))
        sd = dsa_mask(seln_ref, _dot_nt(dq, dkn_ref[...].astype(BF16)))
        _softmax_updates([jnp.where(cn <= t, s, NEG), sd], [ckvn, dvn_ref[...].astype(BF16)], states)
        olat_ref[...] = acca_ref[...] / _lane_tile(la_ref[...], MLA_D_CKV)
        odsa_ref[...] = accd_ref[...] / ld_ref[...]


def _sample_attend(pt, qlat, qr, dq, sel, sel_new, new_rows, caches, n_pages):
    DB, RA, _ = qlat.shape
    RD = dq.shape[1]
    TP = sel.shape[1]
    pages = min(n_pages, SAMPLE_PAGES_PER_STEP)
    n_steps = n_pages // pages
    page_specs = [pl.BlockSpec(memory_space=pl.ANY)] * len(caches)
    per_seq = lambda shape: pl.BlockSpec((None,) + shape, lambda b, j, pt: (b,) + (0,) * len(shape))
    in_specs = [per_seq((RA, MLA_D_CKV)), per_seq((RA, MLA_D_ROPE)), per_seq((RD, LANES)),
                pl.BlockSpec((None, TP, pages * 2 * PAGE_SIZE), lambda b, j, pt: (b, 0, j)),
                per_seq((TP, 2 * PAGE_SIZE)),
                per_seq((PAGE_SIZE, MLA_D_CKV)), per_seq((PAGE_SIZE, MLA_D_ROPE)),
                per_seq((2 * PAGE_SIZE, LANES)), per_seq((2 * PAGE_SIZE, LANES))]
    n_keys = pages * PAGE_SIZE
    grid_spec = pltpu.PrefetchScalarGridSpec(
        num_scalar_prefetch=1,
        grid=(DB, n_steps),
        in_specs=in_specs + page_specs,
        out_specs=[per_seq((RA, MLA_D_CKV)), per_seq((RD, LANES))],
        scratch_shapes=[pltpu.VMEM((RA, LANES), F32), pltpu.VMEM((RA, LANES), F32), pltpu.VMEM((RA, MLA_D_CKV), F32)]
                       + [pltpu.VMEM((RD, LANES), F32)] * 3
                       + [pltpu.VMEM((n_keys, MLA_D_CKV), BF16), pltpu.VMEM((MLA_D_ROPE, n_keys), BF16),
                          pltpu.VMEM((2 * n_keys, LANES), BF16), pltpu.VMEM((2 * n_keys, LANES), BF16)]
                       + [pltpu.VMEM((2, pages) + c.shape[1:], c.dtype) for c in caches]
                       + [pltpu.SemaphoreType.DMA((2, len(caches)))],
    )
    return pl.pallas_call(
        functools.partial(_sample_attend_kernel, pages=pages, n_steps=n_steps),
        grid_spec=grid_spec,
        out_shape=[jax.ShapeDtypeStruct((DB, RA, MLA_D_CKV), F32), jax.ShapeDtypeStruct((DB, RD, LANES), F32)],
        compiler_params=_cparams("arbitrary", "arbitrary"),
    )(pt, qlat, qr, dq, sel, sel_new, *new_rows, *caches)


def _uv_kernel(o_ref, w_ref, out_ref):
    out_ref[...] = _dot(o_ref[...], w_ref[...]).astype(out_ref.dtype)


def _mla_up_v(olat_h, wuv_t):
    H, T, C = olat_h.shape
    return pl.pallas_call(
        _uv_kernel,
        grid=(H,),
        in_specs=[pl.BlockSpec((None, T, C), lambda h: (h, 0, 0)), pl.BlockSpec((None, C, MLA_D_V), lambda h: (h, 0, 0))],
        out_specs=pl.BlockSpec((T, MLA_D_V), lambda h: (0, h)),
        out_shape=jax.ShapeDtypeStruct((T, H * MLA_D_V), BF16),
        compiler_params=_cparams("parallel"),
    )(olat_h, wuv_t)


def _rope_tables(pos):
    pos = pos.astype(F32)
    T = pos.shape[0]

    def tabs(rot, period):
        half = rot // 2
        inv_freq = ROPE_THETA ** (-jnp.arange(half, dtype=F32) / half)
        ang = pos[:, None] * inv_freq[None, :]
        cos, sin = jnp.cos(ang), jnp.sin(ang)
        one, zero = jnp.ones((T, period - rot), F32), jnp.zeros((T, period - rot), F32)
        zh = jnp.zeros((T, half), F32)
        return (jnp.concatenate([cos, cos, one], 1), jnp.concatenate([-sin, zh, zero], 1),
                jnp.concatenate([zh, sin, zero], 1))

    rep = lambda ts, period: [jnp.tile(t, (1, LANES // period)) for t in ts]
    mla, idx = tabs(MLA_D_ROPE, MLA_D_ROPE), tabs(IDX_ROT, IDX_DIM)
    z = jnp.zeros((T, LANES // 2), F32)
    kri = [jnp.concatenate([mla[0], idx[0]], 1), jnp.concatenate([mla[1], z], 1), jnp.concatenate([mla[2], z], 1),
           jnp.concatenate([z, idx[1]], 1), jnp.concatenate([z, idx[2]], 1)]
    return jnp.concatenate(rep(mla, MLA_D_ROPE) + rep(tabs(DSA_ROT, LANES), LANES) + rep(idx, IDX_DIM) + kri, axis=1)


def _permute_w_in(w_in):
    offs, off = {}, 0
    for name, w in (("mla_cq", MLA_D_CQ), ("mla_ckv", MLA_D_CKV), ("mla_kr", MLA_D_ROPE),
                    ("dsa_q", DSA_HEADS * DSA_HEAD_DIM), ("dsa_k", DSA_KV_HEADS * DSA_HEAD_DIM),
                    ("dsa_v", DSA_KV_HEADS * DSA_HEAD_DIM), ("idx_q", IDX_HEADS * IDX_DIM), ("idx_w", IDX_HEADS),
                    ("idx_k", IDX_DIM), ("mem_q", MEM_HEADS * MEM_HEAD_DIM), ("gates", N_BRANCH * D_MODEL)):
        offs[name] = (off, w)
        off += w
    w_t = jnp.swapaxes(w_in, 0, 1)
    col = lambda n: w_t[offs[n][0]: offs[n][0] + offs[n][1]]
    pad = jnp.zeros((LANES - IDX_HEADS, w_in.shape[0]), w_in.dtype)
    return jnp.concatenate([col("mla_cq"), col("mla_ckv"), col("dsa_q"), col("dsa_k"), col("dsa_v"), col("idx_q"),
                            col("mem_q"), col("mla_kr"), col("idx_k"), col("idx_w"), pad, col("gates")], axis=0)


def kernel(x_prompt, x_sample, cache_mla_ckv, cache_mla_krope, cache_dsa_k, cache_dsa_v, cache_idx_k, cache_mem_k,
           cache_mem_v, page_table, mem_prompt, g_ffn1, w_ffn1_gate, w_ffn1_up, w_ffn1_down, g_mix, w_in, g_mla_cq,
           w_mla_uq, g_mla_q, w_mla_uk, w_mla_uv, g_mla_ckv, g_mla_kr, g_dsa_q, g_dsa_k, g_mem_in, w_mem_kv,
           g_mem_q, g_mem_k, w_o_mla, w_o_dsa, w_o_mem, w_out, g_ffn2, w_ffn2_gate, w_ffn2_up, w_ffn2_down):
    B, S, D = x_prompt.shape
    DB, TS, _ = x_sample.shape
    depth = g_mix.shape[0]
    n_pages = page_table.shape[1]
    past = n_pages * PAGE_SIZE
    TP = SUBLANES
    xp = x_prompt.reshape(B * S, D)
    xs = x_sample.reshape(DB * TS, D)
    pt = page_table.reshape(-1)
    rt_p = _rope_tables(jnp.arange(S))
    rt_s = jnp.tile(_rope_tables(past + jnp.arange(TS)), (DB, 1))
    row = lambda g: g.reshape(1, -1).astype(F32)
    b16 = lambda w: w.astype(BF16)
    st = {k: [] for k in ("ckv_p", "kr_p", "dk_p", "dv_p", "ik_p", "mk_p", "mv_p", "ckv_s", "kr_s", "dk_s", "dv_s", "ik_s")}

    for l in range(depth):
        ffn1 = (row(g_ffn1[l]), b16(w_ffn1_gate[l]), b16(w_ffn1_up[l]), b16(w_ffn1_down[l]))
        ffn2 = (row(g_ffn2[l]), b16(w_ffn2_gate[l]), b16(w_ffn2_up[l]), b16(w_ffn2_down[l]))
        w_in_p = b16(_permute_w_in(w_in[l]))
        gq = g_mla_q[l]
        gains = (row(g_mla_cq[l]), row(gq[:MLA_D_NOPE]), row(jnp.tile(gq[MLA_D_NOPE:], 2)), row(g_mla_ckv[l]),
                 row(jnp.concatenate([g_mla_kr[l], jnp.ones((LANES - MLA_D_ROPE,), F32)])),
                 row(g_dsa_q[l]), row(g_dsa_k[l]), row(g_mem_q[l]))
        wuq = w_mla_uq[l].reshape(MLA_D_CQ, MLA_HEADS, MLA_D_QK)
        wuq = b16(jnp.concatenate([wuq[:, :, :MLA_D_NOPE].reshape(MLA_D_CQ, -1),
                                   wuq[:, :, MLA_D_NOPE:].reshape(MLA_D_CQ, -1)], axis=1))
        wuk, wuv = w_mla_uk[l], w_mla_uv[l]
        w_kv = b16(jnp.concatenate([wuk.transpose(2, 0, 1).reshape(MLA_D_CKV, -1),
                                    wuv.transpose(2, 0, 1).reshape(MLA_D_CKV, -1)], axis=1))
        merge_w = (b16(w_o_mla[l]), b16(w_o_dsa[l]), b16(w_o_mem[l]), b16(w_out[l]))

        mk_p, mv_p = _memory_kv(mem_prompt.reshape(-1, D), row(g_mem_in[l]), b16(w_mem_kv[l]), row(g_mem_k[l]))
        M = mem_prompt.shape[1]
        xp = _ffn_half(xp, *ffn1)
        c = _project(xp, row(g_mix[l]), w_in_p)
        (ckv, kr, dk, dv, ik, iw, qcat, kcat, vb, dq, dkb, dvb, iq, ikb, mq) = _epilogue(
            c, rt_p, gains, wuq, w_kv, sample=False)
        o_mla = _mla_prompt(qcat, kcat, vb, B, S)
        iq3 = iq.reshape(B * S, IDX_HEADS, IDX_DIM).transpose(1, 0, 2)
        o_dsa = _dsa_prompt(iq3, iw, ikb, dq, dkb, dvb, B, S)
        o_mem = _mem_attend(mq.reshape(B, S, -1), b16(mk_p).reshape(B, M, -1),
                            b16(mv_p).reshape(B, M, -1)).reshape(B * S, -1)
        xp = _merge(xp, o_mla, o_dsa, o_mem, c, *merge_w)
        xp = _ffn_half(xp, *ffn2)
        st["ckv_p"].append(ckv.reshape(B, S, MLA_D_CKV))
        st["kr_p"].append(kr.reshape(B, S, MLA_D_ROPE))
        st["dk_p"].append(dk.reshape(B, S, DSA_KV_HEADS, DSA_HEAD_DIM))
        st["dv_p"].append(dv.reshape(B, S, DSA_KV_HEADS, DSA_HEAD_DIM))
        st["ik_p"].append(ik.reshape(B, S, IDX_DIM))
        st["mk_p"].append(mk_p.reshape(B, M, MEM_HEADS, MEM_HEAD_DIM))
        st["mv_p"].append(mv_p.reshape(B, M, MEM_HEADS, MEM_HEAD_DIM))

        xs = _ffn_half(xs, *ffn1)
        c = _project(xs, row(g_mix[l]), w_in_p)
        (ckv, kr, dk, dv, ik, iw, qlat, qr, dq, iq, mq) = _epilogue(c, rt_s, gains, wuq, b16(wuk), sample=True)
        pad_t = lambda a: jnp.pad(a.reshape(DB, TS, -1), ((0, 0), (0, PAGE_SIZE - TS), (0, 0)))
        ksel = min(TOPK_MAX, (past + TS) // 4)
        scores = _sample_scores(pt, iq.reshape(DB, TS * IDX_HEADS, IDX_DIM),
                                iw[:, :IDX_HEADS].reshape(DB, TS * IDX_HEADS, 1),
                                pad_t(ik), jnp.swapaxes(cache_idx_k[l], 1, 2), n_pages)
        sel = _sample_select(scores.reshape(DB * TS, -1), past, ksel, TS).reshape(DB, TS, -1)
        sel = jnp.pad(sel, ((0, 0), (0, TP - TS), (0, 0)))
        dq_s = jnp.pad(dq.reshape(DB, TS, DSA_KV_HEADS, DSA_GROUP, LANES).transpose(0, 2, 3, 1, 4),
                       ((0, 0), (0, 0), (0, 0), (0, TP - TS), (0, 0))).reshape(DB, DSA_HEADS * TP, LANES)
        sel2 = jnp.repeat(sel, DSA_KV_HEADS, axis=2)
        as_page = lambda a: pad_t(a).reshape(DB, DSA_KV_HEADS * PAGE_SIZE, LANES)
        olat, odsa = _sample_attend(
            pt, qlat.reshape(DB, TS * MLA_HEADS, MLA_D_CKV), qr.reshape(DB, TS * MLA_HEADS, MLA_D_ROPE), dq_s,
            sel2[:, :, :DSA_KV_HEADS * past], sel2[:, :, DSA_KV_HEADS * past:],
            (pad_t(ckv), pad_t(kr), as_page(dk), as_page(dv)),
            (cache_mla_ckv[l], jnp.swapaxes(cache_mla_krope[l], 1, 2),
             cache_dsa_k[l].reshape(-1, DSA_KV_HEADS * PAGE_SIZE, LANES),
             cache_dsa_v[l].reshape(-1, DSA_KV_HEADS * PAGE_SIZE, LANES)), n_pages)
        olat_h = b16(olat.reshape(DB * TS, MLA_HEADS, MLA_D_CKV).transpose(1, 0, 2))
        o_mla = _mla_up_v(olat_h, b16(wuv.transpose(0, 2, 1)))
        o_dsa = b16(odsa.reshape(DB, DSA_KV_HEADS, DSA_GROUP, TP, LANES)[:, :, :, :TS]
                    .transpose(0, 3, 1, 2, 4).reshape(DB * TS, DSA_HEADS * LANES))
        mq_s = jnp.pad(mq.reshape(DB, TS, -1), ((0, 0), (0, TP - TS), (0, 0)))
        mem_b16 = lambda c: b16(c[l].reshape(DB, c.shape[2], -1))
        o_mem = _mem_attend(mq_s, mem_b16(cache_mem_k), mem_b16(cache_mem_v))[:, :TS].reshape(DB * TS, -1)
        xs = _merge(xs, o_mla, o_dsa, o_mem, c, *merge_w)
        xs = _ffn_half(xs, *ffn2)
        st["ckv_s"].append(ckv.reshape(DB, TS, MLA_D_CKV))
        st["kr_s"].append(kr.reshape(DB, TS, MLA_D_ROPE))
        st["dk_s"].append(dk.reshape(DB, TS, DSA_KV_HEADS, DSA_HEAD_DIM))
        st["dv_s"].append(dv.reshape(DB, TS, DSA_KV_HEADS, DSA_HEAD_DIM))
        st["ik_s"].append(ik.reshape(DB, TS, IDX_DIM))

    return (xp.reshape(B, S, D), xs.reshape(DB, TS, D),
            jnp.stack(st["ckv_p"]), jnp.stack(st["kr_p"]), jnp.stack(st["dk_p"]), jnp.stack(st["dv_p"]),
            jnp.stack(st["ik_p"]), jnp.stack(st["mk_p"]), jnp.stack(st["mv_p"]),
            jnp.stack(st["ckv_s"]), jnp.stack(st["kr_s"]), jnp.stack(st["dk_s"]), jnp.stack(st["dv_s"]),
            jnp.stack(st["ik_s"]))
```

```python
import functools

import jax
import jax.numpy as jnp
from jax import lax
from jax.experimental import pallas as pl
from jax.experimental.pallas import tpu as pltpu

F32, BF16, I32 = jnp.float32, jnp.bfloat16, jnp.int32

D_MODEL = 2048
D_FF = 5632
PAST_LEN = 8192
PAGE_SIZE = 128
ROPE_THETA = 500000.0
EPS = 1e-6
MLA_HEADS, MLA_D_NOPE, MLA_D_ROPE, MLA_D_V, MLA_D_CQ, MLA_D_CKV = 8, 128, 64, 128, 768, 512
MLA_D_QK = MLA_D_NOPE + MLA_D_ROPE
DSA_HEADS, DSA_KV_HEADS, DSA_HEAD_DIM = 8, 2, 128
DSA_GROUP = DSA_HEADS // DSA_KV_HEADS
DSA_ROT = DSA_HEAD_DIM // 4
IDX_HEADS, IDX_DIM = 16, 64
IDX_ROT = IDX_DIM // 4
TOPK_MAX = 256
MEM_HEADS, MEM_HEAD_DIM = 4, 256
N_BRANCH = 3

LANES = 128
SUBLANES = 8
NEG = -1e30
INT_MIN = -2 ** 31
VMEM_LIMIT = 56 * 1024 * 1024

MLA_SCALE = MLA_D_QK ** -0.5
DSA_SCALE = DSA_HEAD_DIM ** -0.5
MEM_SCALE = MEM_HEAD_DIM ** -0.5

C_CQ = 0
C_CKV = C_CQ + MLA_D_CQ
C_DQ = C_CKV + MLA_D_CKV
C_DK = C_DQ + DSA_HEADS * DSA_HEAD_DIM
C_DV = C_DK + DSA_KV_HEADS * DSA_HEAD_DIM
C_IQ = C_DV + DSA_KV_HEADS * DSA_HEAD_DIM
C_MQ = C_IQ + IDX_HEADS * IDX_DIM
C_KRI = C_MQ + MEM_HEADS * MEM_HEAD_DIM
C_IW = C_KRI + LANES
C_GATE = C_IW + LANES
RT_MLA, RT_DSA, RT_IDX, RT_KRI, RT_SLOTS = 0, 3, 6, 9, 14

NT_DIMS = (((1,), (1,)), ((), ()))
FLASH_HEAD_GROUP = 4
SAMPLE_PAGES_PER_STEP = 16
SAMPLE_CHUNK_PAGES = 4

def _cparams(*sem):
    return pltpu.CompilerParams(dimension_semantics=sem, vmem_limit_bytes=VMEM_LIMIT)


def _rms(x, g):
    return x * lax.rsqrt(jnp.mean(x * x, axis=-1, keepdims=True) + EPS) * g


def _dot(a, b):
    return jnp.dot(a, b, preferred_element_type=F32)


def _dot_nt(a, b):
    return lax.dot_general(a, b, NT_DIMS, preferred_element_type=F32)


def _roll(x, s):
    return pltpu.roll(x, s % x.shape[-1], axis=x.ndim - 1)


def _lane_tile(x, width):
    n = width // LANES
    return x if n == 1 else jnp.concatenate([x] * n, axis=1)


def _softmax_init(m_ref, l_ref, acc_ref):
    m_ref[...] = jnp.full(m_ref.shape, NEG, F32)
    l_ref[...] = jnp.zeros(l_ref.shape, F32)
    acc_ref[...] = jnp.zeros(acc_ref.shape, F32)


def _softmax_weights(s, m_ref, l_ref):
    m_prev = m_ref[...]
    m_new = jnp.maximum(m_prev, jnp.max(s, axis=1, keepdims=True))
    alpha = jnp.exp(m_prev - m_new)
    p = jnp.exp(s - _lane_tile(m_new, s.shape[1]))
    l_ref[...] = alpha * l_ref[...] + jnp.sum(p, axis=1, keepdims=True)
    m_ref[...] = m_new
    return p.astype(BF16), alpha


def _softmax_accumulate(p, alpha, v, acc_ref):
    acc_ref[...] = _lane_tile(alpha, acc_ref.shape[-1]) * acc_ref[...] + _dot(p, v)


def _softmax_updates(scores, values, states):
    weights = [_softmax_weights(s, m, l) for s, (m, l, _) in zip(scores, states)]
    for (p, alpha), v, (_, _, acc) in zip(weights, values, states):
        _softmax_accumulate(p, alpha, v, acc)


def _ordinal_to_float(u):
    k = u ^ INT_MIN
    return pltpu.bitcast(k ^ ((k >> 31) & 0x7FFFFFFF), F32)


def _ffn_kernel(x_ref, g_ref, wg_ref, wu_ref, wd_ref, o_ref, h_ref):
    @pl.when(pl.program_id(1) == 0)
    def _():
        x = x_ref[...]
        h_ref[...] = _rms(x, g_ref[...]).astype(BF16)
        o_ref[...] = x

    h = h_ref[...]
    a = _dot(h, wg_ref[...])
    b = _dot(h, wu_ref[...])
    act = (a * jax.nn.sigmoid(a) * b).astype(BF16)
    o_ref[...] += 0.5 * _dot(act, wd_ref[...])


def _ffn_half(x, g, wg, wu, wd):
    T, D = x.shape
    F = wg.shape[1]
    tm = min(T, 512)
    tf = min(F, 512)
    return pl.pallas_call(
        _ffn_kernel,
        grid=(T // tm, F // tf),
        in_specs=[
            pl.BlockSpec((tm, D), lambda i, j: (i, 0)),
            pl.BlockSpec((1, D), lambda i, j: (0, 0)),
            pl.BlockSpec((D, tf), lambda i, j: (0, j)),
            pl.BlockSpec((D, tf), lambda i, j: (0, j)),
            pl.BlockSpec((tf, D), lambda i, j: (j, 0)),
        ],
        out_specs=pl.BlockSpec((tm, D), lambda i, j: (i, 0)),
        out_shape=jax.ShapeDtypeStruct((T, D), F32),
        scratch_shapes=[pltpu.VMEM((tm, D), BF16)],
        compiler_params=_cparams("parallel", "arbitrary"),
    )(x, g, wg, wu, wd)


def _proj_kernel(x_ref, g_ref, w_ref, o_ref, h_ref):
    @pl.when(pl.program_id(1) == 0)
    def _():
        h_ref[...] = _rms(x_ref[...], g_ref[...]).astype(BF16)

    o_ref[...] = _dot_nt(h_ref[...], w_ref[...])


def _project(x, g, w_t):
    T, D = x.shape
    N = w_t.shape[0]
    tm = min(T, 1024)
    tn = 512
    return pl.pallas_call(
        _proj_kernel,
        grid=(T // tm, N // tn),
        in_specs=[
            pl.BlockSpec((tm, D), lambda i, j: (i, 0)),
            pl.BlockSpec((1, D), lambda i, j: (0, 0)),
            pl.BlockSpec((tn, D), lambda i, j: (j, 0)),
        ],
        out_specs=pl.BlockSpec((tm, tn), lambda i, j: (i, j)),
        out_shape=jax.ShapeDtypeStruct((T, N), F32),
        scratch_shapes=[pltpu.VMEM((tm, D), BF16)],
        compiler_params=_cparams("parallel", "arbitrary"),
    )(x, g, w_t)


def _rope3(x, rt_ref, slot, half):
    c = rt_ref[:, slot * LANES:(slot + 1) * LANES]
    s1 = rt_ref[:, (slot + 1) * LANES:(slot + 2) * LANES]
    s2 = rt_ref[:, (slot + 2) * LANES:(slot + 3) * LANES]
    return x * c + _roll(x, -half) * s1 + _roll(x, half) * s2


def _epilogue_kernel(c_ref, rt_ref, gcq_ref, gqn_ref, gqr_ref, gckv_ref, gkr_ref, gdq_ref, gdk_ref, gmq_ref,
                     wuq_ref, wx_ref, *outs, sample):
    if sample:
        ckv_o, kr_o, dk_o, dv_o, ik_o, iw_o, qlat_o, qr_o, dq_o, iq_o, mq_o = outs
    else:
        (ckv_o, kr_o, dk_o, dv_o, ik_o, iw_o, qcat_o, kcat_o, vb_o, dq_o, dkb_o, dvb_o, iq_o, ikb_o, mq_o) = outs
    tm = c_ref.shape[0]
    lo = lax.broadcasted_iota(I32, (tm, LANES), 1) < MLA_D_ROPE

    cqn = _rms(c_ref[:, C_CQ:C_CQ + MLA_D_CQ], gcq_ref[...]).astype(BF16)
    q = _dot(cqn, wuq_ref[...])
    rope0 = MLA_HEADS * MLA_D_NOPE
    for p in range(MLA_HEADS // 2):
        xr = q[:, rope0 + p * LANES: rope0 + (p + 1) * LANES]
        x2 = xr * xr
        ss_rope = (jnp.sum(jnp.where(lo, x2, 0.0), axis=1, keepdims=True),
                   jnp.sum(jnp.where(lo, 0.0, x2), axis=1, keepdims=True))
        inv = []
        for k in range(2):
            h = 2 * p + k
            qn = q[:, h * MLA_D_NOPE:(h + 1) * MLA_D_NOPE]
            ss = jnp.sum(qn * qn, axis=1, keepdims=True) + ss_rope[k]
            inv.append(lax.rsqrt(ss * (1.0 / MLA_D_QK) + EPS))
            qn = (qn * inv[k] * gqn_ref[...] * MLA_SCALE).astype(BF16)
            if sample:
                qlat_o[:, h * MLA_D_CKV:(h + 1) * MLA_D_CKV] = _dot(qn, wx_ref[h]).astype(BF16)
            else:
                qcat_o[:, h * 2 * LANES: h * 2 * LANES + LANES] = qn
        xr = xr * jnp.where(lo, inv[0], inv[1]) * gqr_ref[...] * MLA_SCALE
        xr = _rope3(xr, rt_ref, RT_MLA, MLA_D_ROPE // 2)
        if sample:
            qr_o[:, p * LANES:(p + 1) * LANES] = xr.astype(BF16)
        else:
            qcat_o[:, (4 * p + 1) * LANES:(4 * p + 2) * LANES] = jnp.where(lo, xr, 0.0).astype(BF16)
            qcat_o[:, (4 * p + 3) * LANES:(4 * p + 4) * LANES] = jnp.where(lo, _roll(xr, MLA_D_ROPE), 0.0).astype(BF16)

    ckv = _rms(c_ref[:, C_CKV:C_CKV + MLA_D_CKV], gckv_ref[...])
    ckv_o[...] = ckv
    x = c_ref[:, C_KRI:C_KRI + LANES]
    ssk = jnp.sum(jnp.where(lo, x * x, 0.0), axis=1, keepdims=True)
    x = jnp.where(lo, x * lax.rsqrt(ssk * (1.0 / MLA_D_ROPE) + EPS) * gkr_ref[...], x)
    s = RT_KRI
    kri = (x * rt_ref[:, s * LANES:(s + 1) * LANES]
           + _roll(x, -(MLA_D_ROPE // 2)) * rt_ref[:, (s + 1) * LANES:(s + 2) * LANES]
           + _roll(x, MLA_D_ROPE // 2) * rt_ref[:, (s + 2) * LANES:(s + 3) * LANES]
           + _roll(x, -(IDX_ROT // 2)) * rt_ref[:, (s + 3) * LANES:(s + 4) * LANES]
           + _roll(x, IDX_ROT // 2) * rt_ref[:, (s + 4) * LANES:(s + 5) * LANES])
    kr_o[...] = kri[:, :MLA_D_ROPE]
    ik_o[...] = kri[:, MLA_D_ROPE:]
    if not sample:
        ikb_o[...] = kri[:, MLA_D_ROPE:].astype(BF16)
        kv = _dot(ckv.astype(BF16), wx_ref[...])
        vb_o[...] = kv[:, MLA_HEADS * MLA_D_NOPE:].astype(BF16)
        krb = jnp.where(lo, kri, 0.0).astype(BF16)
        for h in range(MLA_HEADS):
            kcat_o[:, h * 2 * LANES: h * 2 * LANES + LANES] = kv[:, h * MLA_D_NOPE:(h + 1) * MLA_D_NOPE].astype(BF16)
            kcat_o[:, h * 2 * LANES + LANES:(h + 1) * 2 * LANES] = krb

    for h in range(DSA_HEADS):
        x = _rms(c_ref[:, C_DQ + h * LANES: C_DQ + (h + 1) * LANES], gdq_ref[...])
        dq_o[:, h * LANES:(h + 1) * LANES] = (_rope3(x, rt_ref, RT_DSA, DSA_ROT // 2) * DSA_SCALE).astype(BF16)
    for g in range(DSA_KV_HEADS):
        x = _rms(c_ref[:, C_DK + g * LANES: C_DK + (g + 1) * LANES], gdk_ref[...])
        x = _rope3(x, rt_ref, RT_DSA, DSA_ROT // 2)
        dk_o[:, g * LANES:(g + 1) * LANES] = x
        if not sample:
            dkb_o[:, g * LANES:(g + 1) * LANES] = x.astype(BF16)
    x = c_ref[:, C_DV:C_DV + DSA_KV_HEADS * DSA_HEAD_DIM]
    dv_o[...] = x
    if not sample:
        dvb_o[...] = x.astype(BF16)

    for p in range(IDX_HEADS // 2):
        x = c_ref[:, C_IQ + p * LANES: C_IQ + (p + 1) * LANES]
        iq_o[:, p * LANES:(p + 1) * LANES] = _rope3(x, rt_ref, RT_IDX, IDX_ROT // 2).astype(BF16)
    iw_o[...] = c_ref[:, C_IW:C_IW + LANES]

    for h in range(MEM_HEADS):
        x = _rms(c_ref[:, C_MQ + h * MEM_HEAD_DIM: C_MQ + (h + 1) * MEM_HEAD_DIM], gmq_ref[...])
        mq_o[:, h * MEM_HEAD_DIM:(h + 1) * MEM_HEAD_DIM] = (x * MEM_SCALE).astype(BF16)


def _epilogue(c, rt, gains, wuq, wx, sample):
    T = c.shape[0]
    tm = min(T, 256)
    n_rt = rt.shape[0] // tm
    row = lambda n: pl.BlockSpec((tm, n), lambda i: (i, 0))
    full = lambda a: pl.BlockSpec(a.shape, lambda i: (0,) * a.ndim)
    widths_f32 = [MLA_D_CKV, MLA_D_ROPE, 2 * LANES, 2 * LANES, IDX_DIM, LANES]
    if sample:
        widths_b16 = [MLA_HEADS * MLA_D_CKV, MLA_HEADS * MLA_D_ROPE, 8 * LANES, 8 * LANES, 8 * LANES]
    else:
        widths_b16 = [16 * LANES, 16 * LANES, 8 * LANES, 8 * LANES, 2 * LANES, 2 * LANES, 8 * LANES, IDX_DIM, 8 * LANES]
    out_shape = ([jax.ShapeDtypeStruct((T, n), F32) for n in widths_f32]
                 + [jax.ShapeDtypeStruct((T, n), BF16) for n in widths_b16])
    return pl.pallas_call(
        functools.partial(_epilogue_kernel, sample=sample),
        grid=(T // tm,),
        in_specs=[pl.BlockSpec((tm, C_GATE), lambda i: (i, 0)),
                  pl.BlockSpec((tm, RT_SLOTS * LANES), lambda i: (i % n_rt, 0))]
                 + [full(g) for g in gains] + [full(wuq), full(wx)],
        out_specs=[row(n) for n in widths_f32 + widths_b16],
        out_shape=out_shape,
        compiler_params=_cparams("parallel"),
    )(c, rt, *gains, wuq, wx)


def _mla_flash_kernel(q_ref, k_ref, v_ref, o_ref, m_ref, l_ref, acc_ref):
    i, j = pl.program_id(1), pl.program_id(2)
    tq, tk = q_ref.shape[0], k_ref.shape[0]
    qk, dv = 2 * LANES, MLA_D_V

    @pl.when(j == 0)
    def _():
        _softmax_init(m_ref, l_ref, acc_ref)

    def step(mask):
        for h0 in range(0, MLA_HEADS, FLASH_HEAD_GROUP):
            heads = range(h0, h0 + FLASH_HEAD_GROUP)
            scores = [_dot_nt(q_ref[:, h * qk:(h + 1) * qk], k_ref[:, h * qk:(h + 1) * qk]) for h in heads]
            if mask is not None:
                scores = [jnp.where(mask, s, NEG) for s in scores]
            _softmax_updates(scores, [v_ref[:, h * dv:(h + 1) * dv] for h in heads],
                             [(m_ref.at[h], l_ref.at[h], acc_ref.at[h]) for h in heads])

    @pl.when(j < i)
    def _():
        step(None)

    @pl.when(j == i)
    def _():
        step(lax.broadcasted_iota(I32, (tq, tk), 1) <= lax.broadcasted_iota(I32, (tq, tk), 0))
        for h in range(MLA_HEADS):
            o_ref[:, h * dv:(h + 1) * dv] = (acc_ref[h] / l_ref[h]).astype(o_ref.dtype)


def _mla_prompt(qcat, kcat, vb, B, T):
    tq = min(T, 512)
    nq = T // tq
    H = MLA_HEADS
    return pl.pallas_call(
        _mla_flash_kernel,
        grid=(B, nq, nq),
        in_specs=[
            pl.BlockSpec((tq, H * 2 * LANES), lambda b, i, j: (b * nq + i, 0)),
            pl.BlockSpec((tq, H * 2 * LANES), lambda b, i, j: (b * nq + jnp.minimum(i, j), 0)),
            pl.BlockSpec((tq, H * MLA_D_V), lambda b, i, j: (b * nq + jnp.minimum(i, j), 0)),
        ],
        out_specs=pl.BlockSpec((tq, H * MLA_D_V), lambda b, i, j: (b * nq + i, 0)),
        out_shape=jax.ShapeDtypeStruct((B * T, H * MLA_D_V), BF16),
        scratch_shapes=[pltpu.VMEM((H, tq, LANES), F32), pltpu.VMEM((H, tq, LANES), F32),
                        pltpu.VMEM((H, tq, MLA_D_V), F32)],
        compiler_params=_cparams("parallel", "parallel", "arbitrary"),
    )(qcat, kcat, vb)


def _select_threshold(count_fn, shape, ksel, n_index_bits):
    def value_step(s, prefix):
        cand = prefix | lax.shift_left(jnp.int32(1), 31 - s)
        n = count_fn(lambda x, col, t: x >= t, _ordinal_to_float(cand))
        return jnp.where(n >= ksel, cand, prefix)

    thr = _ordinal_to_float(lax.fori_loop(0, 32, value_step, jnp.zeros(shape, I32)))
    all_visible = count_fn(lambda x, col: x > -jnp.inf) <= ksel
    need = ksel - count_fn(lambda x, col, t: x > t, thr)
    n_ge = count_fn(lambda x, col, t: x >= t, thr)

    def search_cut():
        def cut_step(s, cut):
            cand = cut | lax.shift_left(jnp.int32(1), n_index_bits - 1 - s)
            n = count_fn(lambda x, col, t, c: (x == t) & (col < c), thr, cand)
            return jnp.where(n <= need, cand, cut)
        return lax.fori_loop(0, n_index_bits, cut_step, jnp.zeros(shape, I32))

    cut = lax.cond(jnp.max(n_ge) > ksel, search_cut, lambda: jnp.full(shape, 2 ** n_index_bits, I32))
    return all_visible, thr, cut


def _dsa_prompt_kernel(iq_ref, iw_ref, ik_ref, dq_ref, dk_ref, dv_ref, o_ref,
                       key_ref, bias_ref, m_ref, l_ref, acc_ref, *, ck, ksel):
    i = pl.program_id(1)
    tq = dq_ref.shape[0]
    T = ik_ref.shape[0]
    nch = ((i + 1) * tq + ck - 1) // ck
    row = i * tq + lax.broadcasted_iota(I32, (tq, ck), 0)
    col0 = lax.broadcasted_iota(I32, (tq, ck), 1)

    def idx_chunk(c, carry):
        ikc = ik_ref[pl.ds(pl.multiple_of(c * ck, ck), ck), :]
        acc = jnp.zeros((tq, ck), F32)
        for h in range(IDX_HEADS):
            acc = acc + jnp.maximum(_dot_nt(iq_ref[h], ikc), 0.0) * iw_ref[:, h:h + 1]
        key_ref[c] = jnp.where(c * ck + col0 <= row, acc, -jnp.inf)
        return carry

    lax.fori_loop(0, nch, idx_chunk, 0)

    def count_fn(pred, *per_row):
        def body(c, a):
            hit = jnp.where(pred(key_ref[c], c * ck + col0, *per_row), 1.0, 0.0)
            for t in range(ck // LANES):
                a = a + hit[:, t * LANES:(t + 1) * LANES]
            return a
        a = lax.fori_loop(0, nch, body, jnp.zeros((tq, LANES), F32))
        return jnp.sum(a, axis=1, keepdims=True)

    all_visible, thr, cut = _select_threshold(count_fn, (tq, 1), ksel, T.bit_length())

    def bias_chunk(c, carry):
        x = key_ref[c]
        col = c * ck + col0
        sel = (col <= row) & (all_visible | (x > thr) | ((x == thr) & (col < cut)))
        bias_ref[c] = jnp.where(sel, 0.0, NEG)
        return carry

    lax.fori_loop(0, nch, bias_chunk, 0)

    qg = [jnp.concatenate([dq_ref[:, (g * DSA_GROUP + r) * LANES:(g * DSA_GROUP + r + 1) * LANES]
                           for r in range(DSA_GROUP)], axis=0) for g in range(DSA_KV_HEADS)]
    _softmax_init(m_ref, l_ref, acc_ref)

    def attn_chunk(c, carry):
        k0 = pl.multiple_of(c * ck, ck)
        bias = jnp.concatenate([bias_ref[c]] * DSA_GROUP, axis=0)
        heads = range(DSA_KV_HEADS)
        scores = [_dot_nt(qg[g], dk_ref[pl.ds(k0, ck), g * LANES:(g + 1) * LANES]) + bias for g in heads]
        _softmax_updates(scores, [dv_ref[pl.ds(k0, ck), g * LANES:(g + 1) * LANES] for g in heads],
                         [(m_ref.at[g], l_ref.at[g], acc_ref.at[g]) for g in heads])
        return carry

    lax.fori_loop(0, nch, attn_chunk, 0)
    for g in range(DSA_KV_HEADS):
        o = acc_ref[g] / l_ref[g]
        for r in range(DSA_GROUP):
            o_ref[:, (g * DSA_GROUP + r) * LANES:(g * DSA_GROUP + r + 1) * LANES] = (
                o[r * tq:(r + 1) * tq].astype(o_ref.dtype))


def _dsa_prompt(iq3, iw, ikb, dq, dkb, dvb, B, T):
    tq = min(T, 256)
    ck = min(T, 512)
    nq = T // tq
    ksel = min(TOPK_MAX, T // 4)
    return pl.pallas_call(
        functools.partial(_dsa_prompt_kernel, ck=ck, ksel=ksel),
        grid=(B, nq),
        in_specs=[
            pl.BlockSpec((IDX_HEADS, tq, IDX_DIM), lambda b, i: (0, b * nq + i, 0)),
            pl.BlockSpec((tq, LANES), lambda b, i: (b * nq + i, 0)),
            pl.BlockSpec((T, IDX_DIM), lambda b, i: (b, 0)),
            pl.BlockSpec((tq, DSA_HEADS * LANES), lambda b, i: (b * nq + i, 0)),
            pl.BlockSpec((T, DSA_KV_HEADS * LANES), lambda b, i: (b, 0)),
            pl.BlockSpec((T, DSA_KV_HEADS * LANES), lambda b, i: (b, 0)),
        ],
        out_specs=pl.BlockSpec((tq, DSA_HEADS * LANES), lambda b, i: (b * nq + i, 0)),
        out_shape=jax.ShapeDtypeStruct((B * T, DSA_HEADS * LANES), BF16),
        scratch_shapes=[pltpu.VMEM((T // ck, tq, ck), F32), pltpu.VMEM((T // ck, tq, ck), F32)]
                       + [pltpu.VMEM((DSA_KV_HEADS, DSA_GROUP * tq, LANES), F32)] * 3,
        compiler_params=_cparams("parallel", "arbitrary"),
    )(iq3, iw, ikb, dq, dkb, dvb)


def _memkv_kernel(x_ref, g_ref, w_ref, gk_ref, k_ref, v_ref):
    h = _rms(x_ref[...], g_ref[...]).astype(BF16)
    kv = _dot(h, w_ref[...])
    n = MEM_HEADS * MEM_HEAD_DIM
    for hd in range(MEM_HEADS):
        k_ref[:, hd * MEM_HEAD_DIM:(hd + 1) * MEM_HEAD_DIM] = _rms(
            kv[:, hd * MEM_HEAD_DIM:(hd + 1) * MEM_HEAD_DIM], gk_ref[...])
    v_ref[...] = kv[:, n:]


def _memory_kv(mem, g_in, w, g_k):
    T, D = mem.shape
    n = MEM_HEADS * MEM_HEAD_DIM
    tm = min(T, 256)
    return pl.pallas_call(
        _memkv_kernel,
        grid=(T // tm,),
        in_specs=[pl.BlockSpec((tm, D), lambda i: (i, 0)), pl.BlockSpec((1, D), lambda i: (0, 0)),
                  pl.BlockSpec((D, 2 * n), lambda i: (0, 0)), pl.BlockSpec((1, MEM_HEAD_DIM), lambda i: (0, 0))],
        out_specs=[pl.BlockSpec((tm, n), lambda i: (i, 0))] * 2,
        out_shape=[jax.ShapeDtypeStruct((T, n), F32)] * 2,
        compiler_params=_cparams("parallel"),
    )(mem, g_in, w, g_k)


def _mem_attend_kernel(q_ref, k_ref, v_ref, o_ref):
    heads = [slice(h * MEM_HEAD_DIM, (h + 1) * MEM_HEAD_DIM) for h in range(MEM_HEADS)]
    scores = [_dot_nt(q_ref[:, sl], k_ref[:, sl]) for sl in heads]
    probs = []
    for s in scores:
        p = jnp.exp(s - jnp.max(s, axis=1, keepdims=True))
        probs.append((p / jnp.sum(p, axis=1, keepdims=True)).astype(BF16))
    for p, sl in zip(probs, heads):
        o_ref[:, sl] = _dot(p, v_ref[:, sl]).astype(o_ref.dtype)


def _mem_attend(mq, mk, mv):
    G, R, n = mq.shape
    M = mk.shape[1]
    tm = min(R, 512)
    mem = pl.BlockSpec((None, M, n), lambda g, i: (g, 0, 0))
    return pl.pallas_call(
        _mem_attend_kernel,
        grid=(G, R // tm),
        in_specs=[pl.BlockSpec((None, tm, n), lambda g, i: (g, i, 0)), mem, mem],
        out_specs=pl.BlockSpec((None, tm, n), lambda g, i: (g, i, 0)),
        out_shape=jax.ShapeDtypeStruct((G, R, n), BF16),
        compiler_params=_cparams("parallel", "parallel"),
    )(mq, mk, mv)


def _mem_attend_tiled_kernel(q_ref, k_ref, v_ref, o_ref, *, tp):
    R = q_ref.shape[0] // 2
    k, v = k_ref[...].astype(BF16), v_ref[...].astype(BF16)
    a = _dot_nt(q_ref[...], k)
    row = lax.broadcasted_iota(I32, a.shape, 0)
    col = lax.broadcasted_iota(I32, a.shape, 1)
    own = ((col // MEM_HEADS) % 2 == row // R) & (col % MEM_HEADS == (row % R) // tp)
    a = jnp.where(own, a, 0.0)
    s = jnp.where(own[:R], a[:R] + _roll(a[R:], -MEM_HEADS), NEG)
    p = jnp.exp(s - jnp.max(s, axis=1, keepdims=True))
    p = p / jnp.sum(p, axis=1, keepdims=True)
    o_ref[:, :LANES] = _dot(p.astype(BF16), v).astype(o_ref.dtype)
    o_ref[:, LANES:] = _dot(_roll(p, MEM_HEADS).astype(BF16), v).astype(o_ref.dtype)


def _mem_attend_tiled(mq, mk, mv, tp):
    G, R2, _ = mq.shape
    rows = mk.shape[1]
    mem = pl.BlockSpec((None, rows, LANES), lambda g: (g, 0, 0))
    return pl.pallas_call(
        functools.partial(_mem_attend_tiled_kernel, tp=tp),
        grid=(G,),
        in_specs=[pl.BlockSpec((None, R2, LANES), lambda g: (g, 0, 0)), mem, mem],
        out_specs=pl.BlockSpec((None, R2 // 2, MEM_HEAD_DIM), lambda g: (g, 0, 0)),
        out_shape=jax.ShapeDtypeStruct((G, R2 // 2, MEM_HEAD_DIM), BF16),
        compiler_params=_cparams("parallel"),
    )(mq, mk, mv)


def _merge_kernel(x_ref, oa_ref, ob_ref, oc_ref, ga_ref, gb_ref, gc_ref, wa_ref, wb_ref, wc_ref, wo_ref, o_ref):
    @pl.when(pl.program_id(1) == 0)
    def _():
        o_ref[...] = x_ref[...]

    merged = (jax.nn.sigmoid(ga_ref[...]) * _dot(oa_ref[...], wa_ref[...])
              + jax.nn.sigmoid(gb_ref[...]) * _dot(ob_ref[...], wb_ref[...])
              + jax.nn.sigmoid(gc_ref[...]) * _dot(oc_ref[...], wc_ref[...]))
    o_ref[...] += _dot(merged.astype(BF16), wo_ref[...])


def _merge(x, oa, ob, oc, c, wa, wb, wc, wo):
    T, D = x.shape
    n = oa.shape[1]
    tm = min(T, 512)
    tn = min(D, 512)
    g0 = C_GATE // tn
    nb = D // tn
    gate = lambda k: pl.BlockSpec((tm, tn), lambda i, j: (i, g0 + k * nb + j))
    branch_w = pl.BlockSpec((n, tn), lambda i, j: (0, j))
    rows = pl.BlockSpec((tm, n), lambda i, j: (i, 0))
    return pl.pallas_call(
        _merge_kernel,
        grid=(T // tm, nb),
        in_specs=[pl.BlockSpec((tm, D), lambda i, j: (i, 0)), rows, rows, rows, gate(0), gate(1), gate(2),
                  branch_w, branch_w, branch_w, pl.BlockSpec((tn, D), lambda i, j: (j, 0))],
        out_specs=pl.BlockSpec((tm, D), lambda i, j: (i, 0)),
        out_shape=jax.ShapeDtypeStruct((T, D), F32),
        compiler_params=_cparams("parallel", "arbitrary"),
    )(x, oa, ob, oc, c, c, c, wa, wb, wc, wo)


def _sample_scores_kernel(pt_ref, iq_ref, iw_ref, ikn_ref, ikt_hbm, o_ref, buf, sem, *, n_pages, group):
    b = pl.program_id(0)
    slot = b % 2
    R = iq_ref.shape[0]
    TS = R // IDX_HEADS
    iq = iq_ref[...]
    iw = iw_ref[...]

    def page_copies(seq, slot):
        return [pltpu.make_async_copy(ikt_hbm.at[pt_ref[seq * n_pages + k]], buf.at[slot, k], sem.at[slot])
                for k in range(n_pages)]

    @pl.when(b == 0)
    def _():
        for c in page_copies(0, 0):
            c.start()

    @pl.when(b + 1 < pl.num_programs(0))
    def _():
        for c in page_copies(b + 1, 1 - slot):
            c.start()

    for c in page_copies(b, slot):
        c.wait()

    def head_sum(s):
        s = jnp.maximum(s, 0.0) * iw
        return jnp.sum(s.reshape(TS, IDX_HEADS, s.shape[1]), axis=1)

    for g0 in range(0, n_pages, group):
        kt = jnp.concatenate([buf[slot, k].astype(BF16) for k in range(g0, g0 + group)], axis=1)
        o_ref[:, g0 * PAGE_SIZE:(g0 + group) * PAGE_SIZE] = head_sum(_dot(iq, kt))
    o_ref[:, n_pages * PAGE_SIZE:] = head_sum(_dot_nt(iq, ikn_ref[...].astype(BF16)))


def _sample_scores(pt, iq, iw, ik_new, cache_ikt, n_pages):
    DB, R, _ = iq.shape
    TS = R // IDX_HEADS
    W = (n_pages + 1) * PAGE_SIZE
    grid_spec = pltpu.PrefetchScalarGridSpec(
        num_scalar_prefetch=1,
        grid=(DB,),
        in_specs=[pl.BlockSpec((None, R, IDX_DIM), lambda b, pt: (b, 0, 0)),
                  pl.BlockSpec((None, R, 1), lambda b, pt: (b, 0, 0)),
                  pl.BlockSpec((None, PAGE_SIZE, IDX_DIM), lambda b, pt: (b, 0, 0)),
                  pl.BlockSpec(memory_space=pl.ANY)],
        out_specs=pl.BlockSpec((None, TS, W), lambda b, pt: (b, 0, 0)),
        scratch_shapes=[pltpu.VMEM((2, n_pages, IDX_DIM, PAGE_SIZE), cache_ikt.dtype),
                        pltpu.SemaphoreType.DMA((2,))],
    )
    return pl.pallas_call(
        functools.partial(_sample_scores_kernel, n_pages=n_pages, group=min(n_pages, 8)),
        grid_spec=grid_spec,
        out_shape=jax.ShapeDtypeStruct((DB, TS, W), F32),
        compiler_params=_cparams("arbitrary"),
    )(pt, iq, iw, ik_new, cache_ikt)


def _sample_select_kernel(s_ref, o_ref, *, past, ksel, tp):
    RB, W = s_ref.shape
    t = lax.broadcasted_iota(I32, (RB, W), 0) % tp
    col = lax.broadcasted_iota(I32, (RB, W), 1)
    visible = col <= past + t
    x = jnp.where(visible, s_ref[...], -jnp.inf)

    def count_fn(pred, *per_row):
        hit = jnp.where(pred(x, col, *per_row), 1.0, 0.0)
        a = hit[:, :LANES]
        for c in range(1, W // LANES):
            a = a + hit[:, c * LANES:(c + 1) * LANES]
        return jnp.sum(a, axis=1, keepdims=True)

    all_visible, thr, cut = _select_threshold(count_fn, (RB, 1), ksel, W.bit_length())
    sel = visible & (all_visible | (x > thr) | ((x == thr) & (col < cut)))
    o_ref[...] = jnp.where(sel, 1.0, 0.0).astype(o_ref.dtype)


def _sample_select(scores, past, ksel, tp):
    R, W = scores.shape
    rb = min(R, 64)
    return pl.pallas_call(
        functools.partial(_sample_select_kernel, past=past, ksel=ksel, tp=tp),
        grid=(R // rb,),
        in_specs=[pl.BlockSpec((rb, W), lambda i: (i, 0))],
        out_specs=pl.BlockSpec((rb, W), lambda i: (i, 0)),
        out_shape=jax.ShapeDtypeStruct((R, W), BF16),
        compiler_params=_cparams("parallel"),
    )(scores)


def _sample_attend_kernel(pt_ref, qlat_ref, qr_ref, dq_ref, sel_ref, seln_ref,
                          ckvn_ref, krn_ref, dkn_ref, dvn_ref, ckv_hbm, krt_hbm, dk_hbm, dv_hbm,
                          olat_ref, odsa_ref, ma_ref, la_ref, acca_ref, md_ref, ld_ref, accd_ref,
                          ckv_cat, krt_cat, dk_cat, dv_cat, ckv_buf, krt_buf, dk_buf, dv_buf, sem, *, pages, n_steps):
    b, j = pl.program_id(0), pl.program_id(1)
    step = b * n_steps + j
    n_total = pl.num_programs(0) * n_steps
    slot = step % 2
    RA, RD = qlat_ref.shape[0], dq_ref.shape[0]
    states = [(ma_ref, la_ref, acca_ref), (md_ref, ld_ref, accd_ref)]

    def page_copies(step, slot):
        copies = []
        for k in range(pages):
            page = pt_ref[step * pages + k]
            copies += [pltpu.make_async_copy(hbm.at[page], buf.at[slot, k], sem.at[slot, a])
                       for a, (hbm, buf) in enumerate(((ckv_hbm, ckv_buf), (krt_hbm, krt_buf),
                                                       (dk_hbm, dk_buf), (dv_hbm, dv_buf)))]
        return copies

    @pl.when(step == 0)
    def _():
        for c in page_copies(0, 0):
            c.start()

    @pl.when(step + 1 < n_total)
    def _():
        for c in page_copies(step + 1, 1 - slot):
            c.start()

    for c in page_copies(step, slot):
        c.wait()
    ckv_pages, krt_pages, dk_pages, dv_pages = ([buf.at[slot, k] for k in range(pages)]
                                                for buf in (ckv_buf, krt_buf, dk_buf, dv_buf))

    @pl.when(j == 0)
    def _():
        _softmax_init(ma_ref, la_ref, acca_ref)
        _softmax_init(md_ref, ld_ref, accd_ref)

    qlat, qr, dq = qlat_ref[...], qr_ref[...], dq_ref[...]

    def dsa_mask(flags_ref, s):
        flags = flags_ref[...].astype(F32)
        selected = jnp.concatenate([flags] * (RD // flags.shape[0]), axis=0) > 0.5
        head_of_col = lax.broadcasted_iota(I32, selected.shape, 1) % DSA_KV_HEADS
        head_of_row = lax.broadcasted_iota(I32, selected.shape, 0) // (RD // DSA_KV_HEADS)
        return jnp.where(selected & (head_of_col == head_of_row), s, NEG)

    s_mla, s_dsa = [], []
    for c0 in range(0, pages, SAMPLE_CHUNK_PAGES):
        for k in range(c0, min(c0 + SAMPLE_CHUNK_PAGES, pages)):
            ckv_cat[k * PAGE_SIZE:(k + 1) * PAGE_SIZE, :] = ckv_pages[k][...].astype(BF16)
            krt_cat[:, k * PAGE_SIZE:(k + 1) * PAGE_SIZE] = krt_pages[k][...].astype(BF16)
            dk_cat[k * 2 * PAGE_SIZE:(k + 1) * 2 * PAGE_SIZE, :] = dk_pages[k][...].astype(BF16)
            dv_cat[k * 2 * PAGE_SIZE:(k + 1) * 2 * PAGE_SIZE, :] = dv_pages[k][...].astype(BF16)
        keys = slice(c0 * PAGE_SIZE, min(c0 + SAMPLE_CHUNK_PAGES, pages) * PAGE_SIZE)
        rows = slice(2 * keys.start, 2 * keys.stop)
        s_mla.append(_dot_nt(qlat, ckv_cat[keys, :]) + _dot(qr, krt_cat[:, keys]))
        s_dsa.append(_dot_nt(dq, dk_cat[rows, :]))
    _softmax_updates([jnp.concatenate(s_mla, axis=1), dsa_mask(sel_ref, jnp.concatenate(s_dsa, axis=1))],
                     [ckv_cat[...], dv_cat[...]], states)

    @pl.when(j == n_steps - 1)
    def _():
        t = lax.broadcasted_iota(I32, (RA, PAGE_SIZE), 0) // MLA_HEADS
        cn = lax.broadcasted_iota(I32, (RA, PAGE_SIZE), 1)
        ckvn = ckvn_ref[...].astype(BF16)
        s = _dot_nt(qlat, ckvn) + _dot_nt(qr, krn_ref[...].astype(BF16))
        sd = dsa_mask(seln_ref, _dot_nt(dq, dkn_ref[...].astype(BF16)))
        _softmax_updates([jnp.where(cn <= t, s, NEG), sd], [ckvn, dvn_ref[...].astype(BF16)], states)
        olat_ref[...] = acca_ref[...] / _lane_tile(la_ref[...], MLA_D_CKV)
        odsa_ref[...] = accd_ref[...] / ld_ref[...]


def _sample_attend(pt, qlat, qr, dq, sel, sel_new, new_rows, caches, n_pages):
    DB, RA, _ = qlat.shape
    RD = dq.shape[1]
    TP = sel.shape[1]
    pages = min(n_pages, SAMPLE_PAGES_PER_STEP)
    n_steps = n_pages // pages
    page_specs = [pl.BlockSpec(memory_space=pl.ANY)] * len(caches)
    per_seq = lambda shape: pl.BlockSpec((None,) + shape, lambda b, j, pt: (b,) + (0,) * len(shape))
    in_specs = [per_seq((RA, MLA_D_CKV)), per_seq((RA, MLA_D_ROPE)), per_seq((RD, LANES)),
                pl.BlockSpec((None, TP, pages * 2 * PAGE_SIZE), lambda b, j, pt: (b, 0, j)),
                per_seq((TP, 2 * PAGE_SIZE)),
                per_seq((PAGE_SIZE, MLA_D_CKV)), per_seq((PAGE_SIZE, MLA_D_ROPE)),
                per_seq((2 * PAGE_SIZE, LANES)), per_seq((2 * PAGE_SIZE, LANES))]
    n_keys = pages * PAGE_SIZE
    grid_spec = pltpu.PrefetchScalarGridSpec(
        num_scalar_prefetch=1,
        grid=(DB, n_steps),
        in_specs=in_specs + page_specs,
        out_specs=[per_seq((RA, MLA_D_CKV)), per_seq((RD, LANES))],
        scratch_shapes=[pltpu.VMEM((RA, LANES), F32), pltpu.VMEM((RA, LANES), F32), pltpu.VMEM((RA, MLA_D_CKV), F32)]
                       + [pltpu.VMEM((RD, LANES), F32)] * 3
                       + [pltpu.VMEM((n_keys, MLA_D_CKV), BF16), pltpu.VMEM((MLA_D_ROPE, n_keys), BF16),
                          pltpu.VMEM((2 * n_keys, LANES), BF16), pltpu.VMEM((2 * n_keys, LANES), BF16)]
                       + [pltpu.VMEM((2, pages) + c.shape[1:], c.dtype) for c in caches]
                       + [pltpu.SemaphoreType.DMA((2, len(caches)))],
    )
    return pl.pallas_call(
        functools.partial(_sample_attend_kernel, pages=pages, n_steps=n_steps),
        grid_spec=grid_spec,
        out_shape=[jax.ShapeDtypeStruct((DB, RA, MLA_D_CKV), F32), jax.ShapeDtypeStruct((DB, RD, LANES), F32)],
        compiler_params=_cparams("arbitrary", "arbitrary"),
    )(pt, qlat, qr, dq, sel, sel_new, *new_rows, *caches)


def _uv_kernel(o_ref, w_ref, out_ref):
    out_ref[...] = _dot(o_ref[...], w_ref[...]).astype(out_ref.dtype)


def _mla_up_v(olat_h, wuv_t):
    H, T, C = olat_h.shape
    return pl.pallas_call(
        _uv_kernel,
        grid=(H,),
        in_specs=[pl.BlockSpec((None, T, C), lambda h: (h, 0, 0)), pl.BlockSpec((None, C, MLA_D_V), lambda h: (h, 0, 0))],
        out_specs=pl.BlockSpec((T, MLA_D_V), lambda h: (0, h)),
        out_shape=jax.ShapeDtypeStruct((T, H * MLA_D_V), BF16),
        compiler_params=_cparams("parallel"),
    )(olat_h, wuv_t)


def _rope_tables(pos):
    pos = pos.astype(F32)
    T = pos.shape[0]

    def tabs(rot, period):
        half = rot // 2
        inv_freq = ROPE_THETA ** (-jnp.arange(half, dtype=F32) / half)
        ang = pos[:, None] * inv_freq[None, :]
        cos, sin = jnp.cos(ang), jnp.sin(ang)
        one, zero = jnp.ones((T, period - rot), F32), jnp.zeros((T, period - rot), F32)
        zh = jnp.zeros((T, half), F32)
        return (jnp.concatenate([cos, cos, one], 1), jnp.concatenate([-sin, zh, zero], 1),
                jnp.concatenate([zh, sin, zero], 1))

    rep = lambda ts, period: [jnp.tile(t, (1, LANES // period)) for t in ts]
    mla, idx = tabs(MLA_D_ROPE, MLA_D_ROPE), tabs(IDX_ROT, IDX_DIM)
    z = jnp.zeros((T, LANES // 2), F32)
    kri = [jnp.concatenate([mla[0], idx[0]], 1), jnp.concatenate([mla[1], z], 1), jnp.concatenate([mla[2], z], 1),
           jnp.concatenate([z, idx[1]], 1), jnp.concatenate([z, idx[2]], 1)]
    return jnp.concatenate(rep(mla, MLA_D_ROPE) + rep(tabs(DSA_ROT, LANES), LANES) + rep(idx, IDX_DIM) + kri, axis=1)


def _permute_w_in(w_in):
    offs, off = {}, 0
    for name, w in (("mla_cq", MLA_D_CQ), ("mla_ckv", MLA_D_CKV), ("mla_kr", MLA_D_ROPE),
                    ("dsa_q", DSA_HEADS * DSA_HEAD_DIM), ("dsa_k", DSA_KV_HEADS * DSA_HEAD_DIM),
                    ("dsa_v", DSA_KV_HEADS * DSA_HEAD_DIM), ("idx_q", IDX_HEADS * IDX_DIM), ("idx_w", IDX_HEADS),
                    ("idx_k", IDX_DIM), ("mem_q", MEM_HEADS * MEM_HEAD_DIM), ("gates", N_BRANCH * D_MODEL)):
        offs[name] = (off, w)
        off += w
    w_t = jnp.swapaxes(w_in, 0, 1)
    col = lambda n: w_t[offs[n][0]: offs[n][0] + offs[n][1]]
    pad = jnp.zeros((LANES - IDX_HEADS, w_in.shape[0]), w_in.dtype)
    return jnp.concatenate([col("mla_cq"), col("mla_ckv"), col("dsa_q"), col("dsa_k"), col("dsa_v"), col("idx_q"),
                            col("mem_q"), col("mla_kr"), col("idx_k"), col("idx_w"), pad, col("gates")], axis=0)


def kernel(x_prompt, x_sample, cache_mla_ckv, cache_mla_krope, cache_dsa_k, cache_dsa_v, cache_idx_k, cache_mem_k,
           cache_mem_v, page_table, mem_prompt, g_ffn1, w_ffn1_gate, w_ffn1_up, w_ffn1_down, g_mix, w_in, g_mla_cq,
           w_mla_uq, g_mla_q, w_mla_uk, w_mla_uv, g_mla_ckv, g_mla_kr, g_dsa_q, g_dsa_k, g_mem_in, w_mem_kv,
           g_mem_q, g_mem_k, w_o_mla, w_o_dsa, w_o_mem, w_out, g_ffn2, w_ffn2_gate, w_ffn2_up, w_ffn2_down):
    B, S, D = x_prompt.shape
    DB, TS, _ = x_sample.shape
    depth = g_mix.shape[0]
    n_pages = page_table.shape[1]
    past = n_pages * PAGE_SIZE
    TP = SUBLANES
    xp = x_prompt.reshape(B * S, D)
    xs = x_sample.reshape(DB * TS, D)
    pt = page_table.reshape(-1)
    rt_p = _rope_tables(jnp.arange(S))
    rt_s = jnp.tile(_rope_tables(past + jnp.arange(TS)), (DB, 1))
    row = lambda g: g.reshape(1, -1).astype(F32)
    b16 = lambda w: w.astype(BF16)
    st = {k: [] for k in ("ckv_p", "kr_p", "dk_p", "dv_p", "ik_p", "mk_p", "mv_p", "ckv_s", "kr_s", "dk_s", "dv_s", "ik_s")}

    for l in range(depth):
        ffn1 = (row(g_ffn1[l]), b16(w_ffn1_gate[l]), b16(w_ffn1_up[l]), b16(w_ffn1_down[l]))
        ffn2 = (row(g_ffn2[l]), b16(w_ffn2_gate[l]), b16(w_ffn2_up[l]), b16(w_ffn2_down[l]))
        w_in_p = b16(_permute_w_in(w_in[l]))
        gq = g_mla_q[l]
        gains = (row(g_mla_cq[l]), row(gq[:MLA_D_NOPE]), row(jnp.tile(gq[MLA_D_NOPE:], 2)), row(g_mla_ckv[l]),
                 row(jnp.concatenate([g_mla_kr[l], jnp.ones((LANES - MLA_D_ROPE,), F32)])),
                 row(g_dsa_q[l]), row(g_dsa_k[l]), row(g_mem_q[l]))
        wuq = w_mla_uq[l].reshape(MLA_D_CQ, MLA_HEADS, MLA_D_QK)
        wuq = b16(jnp.concatenate([wuq[:, :, :MLA_D_NOPE].reshape(MLA_D_CQ, -1),
                                   wuq[:, :, MLA_D_NOPE:].reshape(MLA_D_CQ, -1)], axis=1))
        wuk, wuv = w_mla_uk[l], w_mla_uv[l]
        w_kv = b16(jnp.concatenate([wuk.transpose(2, 0, 1).reshape(MLA_D_CKV, -1),
                                    wuv.transpose(2, 0, 1).reshape(MLA_D_CKV, -1)], axis=1))
        merge_w = (b16(w_o_mla[l]), b16(w_o_dsa[l]), b16(w_o_mem[l]), b16(w_out[l]))

        mk_p, mv_p = _memory_kv(mem_prompt.reshape(-1, D), row(g_mem_in[l]), b16(w_mem_kv[l]), row(g_mem_k[l]))
        M = mem_prompt.shape[1]
        xp = _ffn_half(xp, *ffn1)
        c = _project(xp, row(g_mix[l]), w_in_p)
        (ckv, kr, dk, dv, ik, iw, qcat, kcat, vb, dq, dkb, dvb, iq, ikb, mq) = _epilogue(
            c, rt_p, gains, wuq, w_kv, sample=False)
        o_mla = _mla_prompt(qcat, kcat, vb, B, S)
        iq3 = iq.reshape(B * S, IDX_HEADS, IDX_DIM).transpose(1, 0, 2)
        o_dsa = _dsa_prompt(iq3, iw, ikb, dq, dkb, dvb, B, S)
        o_mem = _mem_attend(mq.reshape(B, S, -1), b16(mk_p).reshape(B, M, -1),
                            b16(mv_p).reshape(B, M, -1)).reshape(B * S, -1)
        xp = _merge(xp, o_mla, o_dsa, o_mem, c, *merge_w)
        xp = _ffn_half(xp, *ffn2)
        st["ckv_p"].append(ckv.reshape(B, S, MLA_D_CKV))
        st["kr_p"].append(kr.reshape(B, S, MLA_D_ROPE))
        st["dk_p"].append(dk.reshape(B, S, DSA_KV_HEADS, DSA_HEAD_DIM))
        st["dv_p"].append(dv.reshape(B, S, DSA_KV_HEADS, DSA_HEAD_DIM))
        st["ik_p"].append(ik.reshape(B, S, IDX_DIM))
        st["mk_p"].append(mk_p.reshape(B, M, MEM_HEADS, MEM_HEAD_DIM))
        st["mv_p"].append(mv_p.reshape(B, M, MEM_HEADS, MEM_HEAD_DIM))

        xs = _ffn_half(xs, *ffn1)
        c = _project(xs, row(g_mix[l]), w_in_p)
        (ckv, kr, dk, dv, ik, iw, qlat, qr, dq, iq, mq) = _epilogue(c, rt_s, gains, wuq, b16(wuk), sample=True)
        pad_t = lambda a: jnp.pad(a.reshape(DB, TS, -1), ((0, 0), (0, PAGE_SIZE - TS), (0, 0)))
        ksel = min(TOPK_MAX, (past + TS) // 4)
        scores = _sample_scores(pt, iq.reshape(DB, TS * IDX_HEADS, IDX_DIM),
                                iw[:, :IDX_HEADS].reshape(DB, TS * IDX_HEADS, 1),
                                pad_t(ik), jnp.swapaxes(cache_idx_k[l], 1, 2), n_pages)
        sel = _sample_select(scores.reshape(DB * TS, -1), past, ksel, TS).reshape(DB, TS, -1)
        sel = jnp.pad(sel, ((0, 0), (0, TP - TS), (0, 0)))
        dq_s = jnp.pad(dq.reshape(DB, TS, DSA_KV_HEADS, DSA_GROUP, LANES).transpose(0, 2, 3, 1, 4),
                       ((0, 0), (0, 0), (0, 0), (0, TP - TS), (0, 0))).reshape(DB, DSA_HEADS * TP, LANES)
        sel2 = jnp.repeat(sel, DSA_KV_HEADS, axis=2)
        as_page = lambda a: pad_t(a).reshape(DB, DSA_KV_HEADS * PAGE_SIZE, LANES)
        olat, odsa = _sample_attend(
            pt, qlat.reshape(DB, TS * MLA_HEADS, MLA_D_CKV), qr.reshape(DB, TS * MLA_HEADS, MLA_D_ROPE), dq_s,
            sel2[:, :, :DSA_KV_HEADS * past], sel2[:, :, DSA_KV_HEADS * past:],
            (pad_t(ckv), pad_t(kr), as_page(dk), as_page(dv)),
            (cache_mla_ckv[l], jnp.swapaxes(cache_mla_krope[l], 1, 2),
             cache_dsa_k[l].reshape(-1, DSA_KV_HEADS * PAGE_SIZE, LANES),
             cache_dsa_v[l].reshape(-1, DSA_KV_HEADS * PAGE_SIZE, LANES)), n_pages)
        olat_h = b16(olat.reshape(DB * TS, MLA_HEADS, MLA_D_CKV).transpose(1, 0, 2))
        o_mla = _mla_up_v(olat_h, b16(wuv.transpose(0, 2, 1)))
        o_dsa = b16(odsa.reshape(DB, DSA_KV_HEADS, DSA_GROUP, TP, LANES)[:, :, :, :TS]
                    .transpose(0, 3, 1, 2, 4).reshape(DB * TS, DSA_HEADS * LANES))
        halves = MEM_HEAD_DIM // LANES
        mq_s = jnp.pad(mq.reshape(DB, TS, MEM_HEADS, halves, LANES).transpose(0, 3, 2, 1, 4),
                       ((0, 0), (0, 0), (0, 0), (0, TP - TS), (0, 0))).reshape(DB, halves * MEM_HEADS * TP, LANES)
        mem_rows = lambda c: (c[l].reshape(DB, c.shape[2], MEM_HEADS, halves, LANES).transpose(0, 1, 3, 2, 4)
                              .reshape(DB, -1, LANES))
        o_mem = _mem_attend_tiled(mq_s, mem_rows(cache_mem_k), mem_rows(cache_mem_v), TP)
        o_mem = (o_mem.reshape(DB, MEM_HEADS, TP, MEM_HEAD_DIM)[:, :, :TS].transpose(0, 2, 1, 3)
                 .reshape(DB * TS, MEM_HEADS * MEM_HEAD_DIM))
        xs = _merge(xs, o_mla, o_dsa, o_mem, c, *merge_w)
        xs = _ffn_half(xs, *ffn2)
        st["ckv_s"].append(ckv.reshape(DB, TS, MLA_D_CKV))
        st["kr_s"].append(kr.reshape(DB, TS, MLA_D_ROPE))
        st["dk_s"].append(dk.reshape(DB, TS, DSA_KV_HEADS, DSA_HEAD_DIM))
        st["dv_s"].append(dv.reshape(DB, TS, DSA_KV_HEADS, DSA_HEAD_DIM))
        st["ik_s"].append(ik.reshape(DB, TS, IDX_DIM))

    return (xp.reshape(B, S, D), xs.reshape(DB, TS, D),
            jnp.stack(st["ckv_p"]), jnp.stack(st["kr_p"]), jnp.stack(st["dk_p"]), jnp.stack(st["dv_p"]),
            jnp.stack(st["ik_p"]), jnp.stack(st["mk_p"]), jnp.stack(st["mv_p"]),
            jnp.stack(st["ckv_s"]), jnp.stack(st["kr_s"]), jnp.stack(st["dk_s"]), jnp.stack(st["dv_s"]),
            jnp.stack(st["ik_s"]))
```

```python
import functools

import jax
import jax.numpy as jnp
from jax import lax
from jax.experimental import pallas as pl
from jax.experimental.pallas import tpu as pltpu

F32, BF16, I32 = jnp.float32, jnp.bfloat16, jnp.int32

D_MODEL = 2048
D_FF = 5632
PAST_LEN = 8192
PAGE_SIZE = 128
ROPE_THETA = 500000.0
EPS = 1e-6
MLA_HEADS, MLA_D_NOPE, MLA_D_ROPE, MLA_D_V, MLA_D_CQ, MLA_D_CKV = 8, 128, 64, 128, 768, 512
MLA_D_QK = MLA_D_NOPE + MLA_D_ROPE
DSA_HEADS, DSA_KV_HEADS, DSA_HEAD_DIM = 8, 2, 128
DSA_GROUP = DSA_HEADS // DSA_KV_HEADS
DSA_ROT = DSA_HEAD_DIM // 4
IDX_HEADS, IDX_DIM = 16, 64
IDX_ROT = IDX_DIM // 4
TOPK_MAX = 256
MEM_HEADS, MEM_HEAD_DIM = 4, 256
N_BRANCH = 3

LANES = 128
SUBLANES = 8
NEG = -1e30
INT_MIN = -2 ** 31
VMEM_LIMIT = 56 * 1024 * 1024

MLA_SCALE = MLA_D_QK ** -0.5
DSA_SCALE = DSA_HEAD_DIM ** -0.5
MEM_SCALE = MEM_HEAD_DIM ** -0.5

C_CQ = 0
C_CKV = C_CQ + MLA_D_CQ
C_DQ = C_CKV + MLA_D_CKV
C_DK = C_DQ + DSA_HEADS * DSA_HEAD_DIM
C_DV = C_DK + DSA_KV_HEADS * DSA_HEAD_DIM
C_IQ = C_DV + DSA_KV_HEADS * DSA_HEAD_DIM
C_MQ = C_IQ + IDX_HEADS * IDX_DIM
C_KRI = C_MQ + MEM_HEADS * MEM_HEAD_DIM
C_IW = C_KRI + LANES
C_GATE = C_IW + LANES
RT_MLA, RT_DSA, RT_IDX, RT_KRI, RT_SLOTS = 0, 3, 6, 9, 14

NT_DIMS = (((1,), (1,)), ((), ()))
FLASH_HEAD_GROUP = 4
SAMPLE_PAGES_PER_STEP = 16
SAMPLE_CHUNK_PAGES = 4

def _cparams(*sem):
    return pltpu.CompilerParams(dimension_semantics=sem, vmem_limit_bytes=VMEM_LIMIT)


def _rms(x, g):
    return x * lax.rsqrt(jnp.mean(x * x, axis=-1, keepdims=True) + EPS) * g


def _dot(a, b):
    return jnp.dot(a, b, preferred_element_type=F32)


def _dot_nt(a, b):
    return lax.dot_general(a, b, NT_DIMS, preferred_element_type=F32)


def _roll(x, s):
    return pltpu.roll(x, s % x.shape[-1], axis=x.ndim - 1)


def _lane_tile(x, width):
    n = width // LANES
    return x if n == 1 else jnp.concatenate([x] * n, axis=1)


def _softmax_init(m_ref, l_ref, acc_ref):
    m_ref[...] = jnp.full(m_ref.shape, NEG, F32)
    l_ref[...] = jnp.zeros(l_ref.shape, F32)
    acc_ref[...] = jnp.zeros(acc_ref.shape, F32)


def _softmax_weights(s, m_ref, l_ref):
    m_prev = m_ref[...]
    m_new = jnp.maximum(m_prev, jnp.max(s, axis=1, keepdims=True))
    alpha = jnp.exp(m_prev - m_new)
    p = jnp.exp(s - _lane_tile(m_new, s.shape[1]))
    l_ref[...] = alpha * l_ref[...] + jnp.sum(p, axis=1, keepdims=True)
    m_ref[...] = m_new
    return p.astype(BF16), alpha


def _softmax_accumulate(p, alpha, v, acc_ref):
    acc_ref[...] = _lane_tile(alpha, acc_ref.shape[-1]) * acc_ref[...] + _dot(p, v)


def _softmax_updates(scores, values, states):
    weights = [_softmax_weights(s, m, l) for s, (m, l, _) in zip(scores, states)]
    for (p, alpha), v, (_, _, acc) in zip(weights, values, states):
        _softmax_accumulate(p, alpha, v, acc)


def _ordinal_to_float(u):
    k = u ^ INT_MIN
    return pltpu.bitcast(k ^ ((k >> 31) & 0x7FFFFFFF), F32)


def _ffn_kernel(x_ref, g_ref, wg_ref, wu_ref, wd_ref, o_ref, h_ref):
    @pl.when(pl.program_id(1) == 0)
    def _():
        x = x_ref[...]
        h_ref[...] = _rms(x, g_ref[...]).astype(BF16)
        o_ref[...] = x

    h = h_ref[...]
    a = _dot(h, wg_ref[...])
    b = _dot(h, wu_ref[...])
    act = (a * jax.nn.sigmoid(a) * b).astype(BF16)
    o_ref[...] += 0.5 * _dot(act, wd_ref[...])


def _ffn_half(x, g, wg, wu, wd):
    T, D = x.shape
    F = wg.shape[1]
    tm = min(T, 512)
    tf = min(F, 512)
    return pl.pallas_call(
        _ffn_kernel,
        grid=(T // tm, F // tf),
        in_specs=[
            pl.BlockSpec((tm, D), lambda i, j: (i, 0)),
            pl.BlockSpec((1, D), lambda i, j: (0, 0)),
            pl.BlockSpec((D, tf), lambda i, j: (0, j)),
            pl.BlockSpec((D, tf), lambda i, j: (0, j)),
            pl.BlockSpec((tf, D), lambda i, j: (j, 0)),
        ],
        out_specs=pl.BlockSpec((tm, D), lambda i, j: (i, 0)),
        out_shape=jax.ShapeDtypeStruct((T, D), F32),
        scratch_shapes=[pltpu.VMEM((tm, D), BF16)],
        compiler_params=_cparams("parallel", "arbitrary"),
    )(x, g, wg, wu, wd)


def _proj_kernel(x_ref, g_ref, w_ref, o_ref, h_ref):
    @pl.when(pl.program_id(1) == 0)
    def _():
        h_ref[...] = _rms(x_ref[...], g_ref[...]).astype(BF16)

    o_ref[...] = _dot_nt(h_ref[...], w_ref[...])


def _project(x, g, w_t):
    T, D = x.shape
    N = w_t.shape[0]
    tm = min(T, 1024)
    tn = 512
    return pl.pallas_call(
        _proj_kernel,
        grid=(T // tm, N // tn),
        in_specs=[
            pl.BlockSpec((tm, D), lambda i, j: (i, 0)),
            pl.BlockSpec((1, D), lambda i, j: (0, 0)),
            pl.BlockSpec((tn, D), lambda i, j: (j, 0)),
        ],
        out_specs=pl.BlockSpec((tm, tn), lambda i, j: (i, j)),
        out_shape=jax.ShapeDtypeStruct((T, N), F32),
        scratch_shapes=[pltpu.VMEM((tm, D), BF16)],
        compiler_params=_cparams("parallel", "arbitrary"),
    )(x, g, w_t)


def _rope3(x, rt_ref, slot, half):
    c = rt_ref[:, slot * LANES:(slot + 1) * LANES]
    s1 = rt_ref[:, (slot + 1) * LANES:(slot + 2) * LANES]
    s2 = rt_ref[:, (slot + 2) * LANES:(slot + 3) * LANES]
    return x * c + _roll(x, -half) * s1 + _roll(x, half) * s2


def _epilogue_kernel(c_ref, rt_ref, gcq_ref, gqn_ref, gqr_ref, gckv_ref, gkr_ref, gdq_ref, gdk_ref, gmq_ref,
                     wuq_ref, wx_ref, *outs, sample):
    if sample:
        ckv_o, kr_o, dk_o, dv_o, ik_o, iw_o, qlat_o, qr_o, dq_o, iq_o, mq_o = outs
    else:
        (ckv_o, kr_o, dk_o, dv_o, ik_o, iw_o, qcat_o, kcat_o, vb_o, dq_o, dkb_o, dvb_o, iq_o, ikb_o, mq_o) = outs
    tm = c_ref.shape[0]
    lo = lax.broadcasted_iota(I32, (tm, LANES), 1) < MLA_D_ROPE

    cqn = _rms(c_ref[:, C_CQ:C_CQ + MLA_D_CQ], gcq_ref[...]).astype(BF16)
    q = _dot(cqn, wuq_ref[...])
    rope0 = MLA_HEADS * MLA_D_NOPE
    for p in range(MLA_HEADS // 2):
        xr = q[:, rope0 + p * LANES: rope0 + (p + 1) * LANES]
        x2 = xr * xr
        ss_rope = (jnp.sum(jnp.where(lo, x2, 0.0), axis=1, keepdims=True),
                   jnp.sum(jnp.where(lo, 0.0, x2), axis=1, keepdims=True))
        inv = []
        for k in range(2):
            h = 2 * p + k
            qn = q[:, h * MLA_D_NOPE:(h + 1) * MLA_D_NOPE]
            ss = jnp.sum(qn * qn, axis=1, keepdims=True) + ss_rope[k]
            inv.append(lax.rsqrt(ss * (1.0 / MLA_D_QK) + EPS))
            qn = (qn * inv[k] * gqn_ref[...] * MLA_SCALE).astype(BF16)
            if sample:
                qlat_o[:, h * MLA_D_CKV:(h + 1) * MLA_D_CKV] = _dot(qn, wx_ref[h]).astype(BF16)
            else:
                qcat_o[:, h * 2 * LANES: h * 2 * LANES + LANES] = qn
        xr = xr * jnp.where(lo, inv[0], inv[1]) * gqr_ref[...] * MLA_SCALE
        xr = _rope3(xr, rt_ref, RT_MLA, MLA_D_ROPE // 2)
        if sample:
            qr_o[:, p * LANES:(p + 1) * LANES] = xr.astype(BF16)
        else:
            qcat_o[:, (4 * p + 1) * LANES:(4 * p + 2) * LANES] = jnp.where(lo, xr, 0.0).astype(BF16)
            qcat_o[:, (4 * p + 3) * LANES:(4 * p + 4) * LANES] = jnp.where(lo, _roll(xr, MLA_D_ROPE), 0.0).astype(BF16)

    ckv = _rms(c_ref[:, C_CKV:C_CKV + MLA_D_CKV], gckv_ref[...])
    ckv_o[...] = ckv
    x = c_ref[:, C_KRI:C_KRI + LANES]
    ssk = jnp.sum(jnp.where(lo, x * x, 0.0), axis=1, keepdims=True)
    x = jnp.where(lo, x * lax.rsqrt(ssk * (1.0 / MLA_D_ROPE) + EPS) * gkr_ref[...], x)
    s = RT_KRI
    kri = (x * rt_ref[:, s * LANES:(s + 1) * LANES]
           + _roll(x, -(MLA_D_ROPE // 2)) * rt_ref[:, (s + 1) * LANES:(s + 2) * LANES]
           + _roll(x, MLA_D_ROPE // 2) * rt_ref[:, (s + 2) * LANES:(s + 3) * LANES]
           + _roll(x, -(IDX_ROT // 2)) * rt_ref[:, (s + 3) * LANES:(s + 4) * LANES]
           + _roll(x, IDX_ROT // 2) * rt_ref[:, (s + 4) * LANES:(s + 5) * LANES])
    kr_o[...] = kri[:, :MLA_D_ROPE]
    ik_o[...] = kri[:, MLA_D_ROPE:]
    if not sample:
        ikb_o[...] = kri[:, MLA_D_ROPE:].astype(BF16)
        kv = _dot(ckv.astype(BF16), wx_ref[...])
        vb_o[...] = kv[:, MLA_HEADS * MLA_D_NOPE:].astype(BF16)
        krb = jnp.where(lo, kri, 0.0).astype(BF16)
        for h in range(MLA_HEADS):
            kcat_o[:, h * 2 * LANES: h * 2 * LANES + LANES] = kv[:, h * MLA_D_NOPE:(h + 1) * MLA_D_NOPE].astype(BF16)
            kcat_o[:, h * 2 * LANES + LANES:(h + 1) * 2 * LANES] = krb

    for h in range(DSA_HEADS):
        x = _rms(c_ref[:, C_DQ + h * LANES: C_DQ + (h + 1) * LANES], gdq_ref[...])
        dq_o[:, h * LANES:(h + 1) * LANES] = (_rope3(x, rt_ref, RT_DSA, DSA_ROT // 2) * DSA_SCALE).astype(BF16)
    for g in range(DSA_KV_HEADS):
        x = _rms(c_ref[:, C_DK + g * LANES: C_DK + (g + 1) * LANES], gdk_ref[...])
        x = _rope3(x, rt_ref, RT_DSA, DSA_ROT // 2)
        dk_o[:, g * LANES:(g + 1) * LANES] = x
        if not sample:
            dkb_o[:, g * LANES:(g + 1) * LANES] = x.astype(BF16)
    x = c_ref[:, C_DV:C_DV + DSA_KV_HEADS * DSA_HEAD_DIM]
    dv_o[...] = x
    if not sample:
        dvb_o[...] = x.astype(BF16)

    for p in range(IDX_HEADS // 2):
        x = c_ref[:, C_IQ + p * LANES: C_IQ + (p + 1) * LANES]
        iq_o[:, p * LANES:(p + 1) * LANES] = _rope3(x, rt_ref, RT_IDX, IDX_ROT // 2).astype(BF16)
    iw_o[...] = c_ref[:, C_IW:C_IW + LANES]

    for h in range(MEM_HEADS):
        x = _rms(c_ref[:, C_MQ + h * MEM_HEAD_DIM: C_MQ + (h + 1) * MEM_HEAD_DIM], gmq_ref[...])
        mq_o[:, h * MEM_HEAD_DIM:(h + 1) * MEM_HEAD_DIM] = (x * MEM_SCALE).astype(BF16)


def _epilogue(c, rt, gains, wuq, wx, sample):
    T = c.shape[0]
    tm = min(T, 256)
    n_rt = rt.shape[0] // tm
    row = lambda n: pl.BlockSpec((tm, n), lambda i: (i, 0))
    full = lambda a: pl.BlockSpec(a.shape, lambda i: (0,) * a.ndim)
    widths_f32 = [MLA_D_CKV, MLA_D_ROPE, 2 * LANES, 2 * LANES, IDX_DIM, LANES]
    if sample:
        widths_b16 = [MLA_HEADS * MLA_D_CKV, MLA_HEADS * MLA_D_ROPE, 8 * LANES, 8 * LANES, 8 * LANES]
    else:
        widths_b16 = [16 * LANES, 16 * LANES, 8 * LANES, 8 * LANES, 2 * LANES, 2 * LANES, 8 * LANES, IDX_DIM, 8 * LANES]
    out_shape = ([jax.ShapeDtypeStruct((T, n), F32) for n in widths_f32]
                 + [jax.ShapeDtypeStruct((T, n), BF16) for n in widths_b16])
    return pl.pallas_call(
        functools.partial(_epilogue_kernel, sample=sample),
        grid=(T // tm,),
        in_specs=[pl.BlockSpec((tm, C_GATE), lambda i: (i, 0)),
                  pl.BlockSpec((tm, RT_SLOTS * LANES), lambda i: (i % n_rt, 0))]
                 + [full(g) for g in gains] + [full(wuq), full(wx)],
        out_specs=[row(n) for n in widths_f32 + widths_b16],
        out_shape=out_shape,
        compiler_params=_cparams("parallel"),
    )(c, rt, *gains, wuq, wx)


def _mla_flash_kernel(q_ref, k_ref, v_ref, o_ref, m_ref, l_ref, acc_ref):
    i, j = pl.program_id(1), pl.program_id(2)
    tq, tk = q_ref.shape[0], k_ref.shape[0]
    qk, dv = 2 * LANES, MLA_D_V

    @pl.when(j == 0)
    def _():
        _softmax_init(m_ref, l_ref, acc_ref)

    def step(mask):
        for h0 in range(0, MLA_HEADS, FLASH_HEAD_GROUP):
            heads = range(h0, h0 + FLASH_HEAD_GROUP)
            scores = [_dot_nt(q_ref[:, h * qk:(h + 1) * qk], k_ref[:, h * qk:(h + 1) * qk]) for h in heads]
            if mask is not None:
                scores = [jnp.where(mask, s, NEG) for s in scores]
            _softmax_updates(scores, [v_ref[:, h * dv:(h + 1) * dv] for h in heads],
                             [(m_ref.at[h], l_ref.at[h], acc_ref.at[h]) for h in heads])

    @pl.when(j < i)
    def _():
        step(None)

    @pl.when(j == i)
    def _():
        step(lax.broadcasted_iota(I32, (tq, tk), 1) <= lax.broadcasted_iota(I32, (tq, tk), 0))
        for h in range(MLA_HEADS):
            o_ref[:, h * dv:(h + 1) * dv] = (acc_ref[h] / l_ref[h]).astype(o_ref.dtype)


def _mla_prompt(qcat, kcat, vb, B, T):
    tq = min(T, 512)
    nq = T // tq
    H = MLA_HEADS
    return pl.pallas_call(
        _mla_flash_kernel,
        grid=(B, nq, nq),
        in_specs=[
            pl.BlockSpec((tq, H * 2 * LANES), lambda b, i, j: (b * nq + i, 0)),
            pl.BlockSpec((tq, H * 2 * LANES), lambda b, i, j: (b * nq + jnp.minimum(i, j), 0)),
            pl.BlockSpec((tq, H * MLA_D_V), lambda b, i, j: (b * nq + jnp.minimum(i, j), 0)),
        ],
        out_specs=pl.BlockSpec((tq, H * MLA_D_V), lambda b, i, j: (b * nq + i, 0)),
        out_shape=jax.ShapeDtypeStruct((B * T, H * MLA_D_V), BF16),
        scratch_shapes=[pltpu.VMEM((H, tq, LANES), F32), pltpu.VMEM((H, tq, LANES), F32),
                        pltpu.VMEM((H, tq, MLA_D_V), F32)],
        compiler_params=_cparams("parallel", "parallel", "arbitrary"),
    )(qcat, kcat, vb)


def _select_threshold(count_fn, shape, ksel, n_index_bits):
    def value_step(s, prefix):
        cand = prefix | lax.shift_left(jnp.int32(1), 31 - s)
        n = count_fn(lambda x, col, t: x >= t, _ordinal_to_float(cand))
        return jnp.where(n >= ksel, cand, prefix)

    thr = _ordinal_to_float(lax.fori_loop(0, 32, value_step, jnp.zeros(shape, I32)))
    all_visible = count_fn(lambda x, col: x > -jnp.inf) <= ksel
    need = ksel - count_fn(lambda x, col, t: x > t, thr)
    n_ge = count_fn(lambda x, col, t: x >= t, thr)

    def search_cut():
        def cut_step(s, cut):
            cand = cut | lax.shift_left(jnp.int32(1), n_index_bits - 1 - s)
            n = count_fn(lambda x, col, t, c: (x == t) & (col < c), thr, cand)
            return jnp.where(n <= need, cand, cut)
        return lax.fori_loop(0, n_index_bits, cut_step, jnp.zeros(shape, I32))

    cut = lax.cond(jnp.max(n_ge) > ksel, search_cut, lambda: jnp.full(shape, 2 ** n_index_bits, I32))
    return all_visible, thr, cut


def _dsa_prompt_kernel(iq_ref, iw_ref, ik_ref, dq_ref, dk_ref, dv_ref, o_ref,
                       key_ref, bias_ref, m_ref, l_ref, acc_ref, cnt_ref, *, ck, ksel):
    i = pl.program_id(1)
    tq = dq_ref.shape[0]
    T = ik_ref.shape[0]
    nch = ((i + 1) * tq + ck - 1) // ck
    row = i * tq + lax.broadcasted_iota(I32, (tq, ck), 0)
    col0 = lax.broadcasted_iota(I32, (tq, ck), 1)

    def idx_chunk(c, carry):
        ikc = ik_ref[pl.ds(pl.multiple_of(c * ck, ck), ck), :]
        acc = jnp.zeros((tq, ck), F32)
        for h in range(IDX_HEADS):
            acc = acc + jnp.maximum(_dot_nt(iq_ref[h], ikc), 0.0) * iw_ref[:, h:h + 1]
        key_ref[c] = jnp.where(c * ck + col0 <= row, acc, -jnp.inf)
        return carry

    lax.fori_loop(0, nch, idx_chunk, 0)

    def count_fn(pred, *per_row):
        cnt_ref[...] = jnp.zeros(cnt_ref.shape, F32)

        def body(c, carry):
            hit = jnp.where(pred(key_ref[c], c * ck + col0, *per_row), 1.0, 0.0)
            a = hit[:, :LANES]
            for t in range(1, ck // LANES):
                a = a + hit[:, t * LANES:(t + 1) * LANES]
            cnt_ref[...] += a
            return carry
        lax.fori_loop(0, nch, body, 0)
        return jnp.sum(cnt_ref[...], axis=1, keepdims=True)

    all_visible, thr, cut = _select_threshold(count_fn, (tq, 1), ksel, T.bit_length())

    def bias_chunk(c, carry):
        x = key_ref[c]
        col = c * ck + col0
        sel = (col <= row) & (all_visible | (x > thr) | ((x == thr) & (col < cut)))
        bias_ref[c] = jnp.where(sel, 0.0, NEG)
        return carry

    lax.fori_loop(0, nch, bias_chunk, 0)

    qg = [jnp.concatenate([dq_ref[:, (g * DSA_GROUP + r) * LANES:(g * DSA_GROUP + r + 1) * LANES]
                           for r in range(DSA_GROUP)], axis=0) for g in range(DSA_KV_HEADS)]
    _softmax_init(m_ref, l_ref, acc_ref)

    def attn_chunk(c, carry):
        k0 = pl.multiple_of(c * ck, ck)
        bias = jnp.concatenate([bias_ref[c]] * DSA_GROUP, axis=0)
        heads = range(DSA_KV_HEADS)
        scores = [_dot_nt(qg[g], dk_ref[pl.ds(k0, ck), g * LANES:(g + 1) * LANES]) + bias for g in heads]
        _softmax_updates(scores, [dv_ref[pl.ds(k0, ck), g * LANES:(g + 1) * LANES] for g in heads],
                         [(m_ref.at[g], l_ref.at[g], acc_ref.at[g]) for g in heads])
        return carry

    lax.fori_loop(0, nch, attn_chunk, 0)
    for g in range(DSA_KV_HEADS):
        o = acc_ref[g] / l_ref[g]
        for r in range(DSA_GROUP):
            o_ref[:, (g * DSA_GROUP + r) * LANES:(g * DSA_GROUP + r + 1) * LANES] = (
                o[r * tq:(r + 1) * tq].astype(o_ref.dtype))


def _dsa_prompt(iq3, iw, ikb, dq, dkb, dvb, B, T):
    tq = min(T, 256)
    ck = min(T, 512)
    nq = T // tq
    ksel = min(TOPK_MAX, T // 4)
    return pl.pallas_call(
        functools.partial(_dsa_prompt_kernel, ck=ck, ksel=ksel),
        grid=(B, nq),
        in_specs=[
            pl.BlockSpec((IDX_HEADS, tq, IDX_DIM), lambda b, i: (0, b * nq + i, 0)),
            pl.BlockSpec((tq, LANES), lambda b, i: (b * nq + i, 0)),
            pl.BlockSpec((T, IDX_DIM), lambda b, i: (b, 0)),
            pl.BlockSpec((tq, DSA_HEADS * LANES), lambda b, i: (b * nq + i, 0)),
            pl.BlockSpec((T, DSA_KV_HEADS * LANES), lambda b, i: (b, 0)),
            pl.BlockSpec((T, DSA_KV_HEADS * LANES), lambda b, i: (b, 0)),
        ],
        out_specs=pl.BlockSpec((tq, DSA_HEADS * LANES), lambda b, i: (b * nq + i, 0)),
        out_shape=jax.ShapeDtypeStruct((B * T, DSA_HEADS * LANES), BF16),
        scratch_shapes=[pltpu.VMEM((T // ck, tq, ck), F32), pltpu.VMEM((T // ck, tq, ck), F32)]
                       + [pltpu.VMEM((DSA_KV_HEADS, DSA_GROUP * tq, LANES), F32)] * 3
                       + [pltpu.VMEM((tq, LANES), F32)],
        compiler_params=_cparams("parallel", "arbitrary"),
    )(iq3, iw, ikb, dq, dkb, dvb)


def _memkv_kernel(x_ref, g_ref, w_ref, gk_ref, k_ref, v_ref):
    h = _rms(x_ref[...], g_ref[...]).astype(BF16)
    kv = _dot(h, w_ref[...])
    n = MEM_HEADS * MEM_HEAD_DIM
    for hd in range(MEM_HEADS):
        k_ref[:, hd * MEM_HEAD_DIM:(hd + 1) * MEM_HEAD_DIM] = _rms(
            kv[:, hd * MEM_HEAD_DIM:(hd + 1) * MEM_HEAD_DIM], gk_ref[...])
    v_ref[...] = kv[:, n:]


def _memory_kv(mem, g_in, w, g_k):
    T, D = mem.shape
    n = MEM_HEADS * MEM_HEAD_DIM
    tm = min(T, 256)
    return pl.pallas_call(
        _memkv_kernel,
        grid=(T // tm,),
        in_specs=[pl.BlockSpec((tm, D), lambda i: (i, 0)), pl.BlockSpec((1, D), lambda i: (0, 0)),
                  pl.BlockSpec((D, 2 * n), lambda i: (0, 0)), pl.BlockSpec((1, MEM_HEAD_DIM), lambda i: (0, 0))],
        out_specs=[pl.BlockSpec((tm, n), lambda i: (i, 0))] * 2,
        out_shape=[jax.ShapeDtypeStruct((T, n), F32)] * 2,
        compiler_params=_cparams("parallel"),
    )(mem, g_in, w, g_k)


def _mem_attend_kernel(q_ref, k_ref, v_ref, o_ref):
    heads = [slice(h * MEM_HEAD_DIM, (h + 1) * MEM_HEAD_DIM) for h in range(MEM_HEADS)]
    scores = [_dot_nt(q_ref[:, sl], k_ref[:, sl]) for sl in heads]
    probs = []
    for s in scores:
        p = jnp.exp(s - jnp.max(s, axis=1, keepdims=True))
        probs.append((p / jnp.sum(p, axis=1, keepdims=True)).astype(BF16))
    for p, sl in zip(probs, heads):
        o_ref[:, sl] = _dot(p, v_ref[:, sl]).astype(o_ref.dtype)


def _mem_attend(mq, mk, mv):
    G, R, n = mq.shape
    M = mk.shape[1]
    tm = min(R, 512)
    mem = pl.BlockSpec((None, M, n), lambda g, i: (g, 0, 0))
    return pl.pallas_call(
        _mem_attend_kernel,
        grid=(G, R // tm),
        in_specs=[pl.BlockSpec((None, tm, n), lambda g, i: (g, i, 0)), mem, mem],
        out_specs=pl.BlockSpec((None, tm, n), lambda g, i: (g, i, 0)),
        out_shape=jax.ShapeDtypeStruct((G, R, n), BF16),
        compiler_params=_cparams("parallel", "parallel"),
    )(mq, mk, mv)


def _mem_attend_tiled_kernel(q_ref, k_ref, v_ref, o_ref, *, tp):
    R = q_ref.shape[0] // 2
    k, v = k_ref[...].astype(BF16), v_ref[...].astype(BF16)
    a = _dot_nt(q_ref[...], k)
    row = lax.broadcasted_iota(I32, a.shape, 0)
    col = lax.broadcasted_iota(I32, a.shape, 1)
    own = ((col // MEM_HEADS) % 2 == row // R) & (col % MEM_HEADS == (row % R) // tp)
    a = jnp.where(own, a, 0.0)
    s = jnp.where(own[:R], a[:R] + _roll(a[R:], -MEM_HEADS), NEG)
    p = jnp.exp(s - jnp.max(s, axis=1, keepdims=True))
    p = p / jnp.sum(p, axis=1, keepdims=True)
    o_ref[:, :LANES] = _dot(p.astype(BF16), v).astype(o_ref.dtype)
    o_ref[:, LANES:] = _dot(_roll(p, MEM_HEADS).astype(BF16), v).astype(o_ref.dtype)


def _mem_attend_tiled(mq, mk, mv, tp):
    G, R2, _ = mq.shape
    rows = mk.shape[1]
    mem = pl.BlockSpec((None, rows, LANES), lambda g: (g, 0, 0))
    return pl.pallas_call(
        functools.partial(_mem_attend_tiled_kernel, tp=tp),
        grid=(G,),
        in_specs=[pl.BlockSpec((None, R2, LANES), lambda g: (g, 0, 0)), mem, mem],
        out_specs=pl.BlockSpec((None, R2 // 2, MEM_HEAD_DIM), lambda g: (g, 0, 0)),
        out_shape=jax.ShapeDtypeStruct((G, R2 // 2, MEM_HEAD_DIM), BF16),
        compiler_params=_cparams("parallel"),
    )(mq, mk, mv)


def _merge_kernel(x_ref, oa_ref, ob_ref, oc_ref, ga_ref, gb_ref, gc_ref, wa_ref, wb_ref, wc_ref, wo_ref, o_ref):
    @pl.when(pl.program_id(1) == 0)
    def _():
        o_ref[...] = x_ref[...]

    merged = (jax.nn.sigmoid(ga_ref[...]) * _dot(oa_ref[...], wa_ref[...])
              + jax.nn.sigmoid(gb_ref[...]) * _dot(ob_ref[...], wb_ref[...])
              + jax.nn.sigmoid(gc_ref[...]) * _dot(oc_ref[...], wc_ref[...]))
    o_ref[...] += _dot(merged.astype(BF16), wo_ref[...])


def _merge(x, oa, ob, oc, c, wa, wb, wc, wo):
    T, D = x.shape
    n = oa.shape[1]
    tm = min(T, 512)
    tn = min(D, 512)
    g0 = C_GATE // tn
    nb = D // tn
    gate = lambda k: pl.BlockSpec((tm, tn), lambda i, j: (i, g0 + k * nb + j))
    branch_w = pl.BlockSpec((n, tn), lambda i, j: (0, j))
    rows = pl.BlockSpec((tm, n), lambda i, j: (i, 0))
    return pl.pallas_call(
        _merge_kernel,
        grid=(T // tm, nb),
        in_specs=[pl.BlockSpec((tm, D), lambda i, j: (i, 0)), rows, rows, rows, gate(0), gate(1), gate(2),
                  branch_w, branch_w, branch_w, pl.BlockSpec((tn, D), lambda i, j: (j, 0))],
        out_specs=pl.BlockSpec((tm, D), lambda i, j: (i, 0)),
        out_shape=jax.ShapeDtypeStruct((T, D), F32),
        compiler_params=_cparams("parallel", "arbitrary"),
    )(x, oa, ob, oc, c, c, c, wa, wb, wc, wo)


def _sample_scores_kernel(pt_ref, iq_ref, iw_ref, ikn_ref, ikt_hbm, o_ref, buf, sem, *, n_pages, group):
    b = pl.program_id(0)
    slot = b % 2
    R = iq_ref.shape[0]
    TS = R // IDX_HEADS
    iq = iq_ref[...]
    iw = iw_ref[...]

    def page_copies(seq, slot):
        return [pltpu.make_async_copy(ikt_hbm.at[pt_ref[seq * n_pages + k]], buf.at[slot, k], sem.at[slot])
                for k in range(n_pages)]

    @pl.when(b == 0)
    def _():
        for c in page_copies(0, 0):
            c.start()

    @pl.when(b + 1 < pl.num_programs(0))
    def _():
        for c in page_copies(b + 1, 1 - slot):
            c.start()

    for c in page_copies(b, slot):
        c.wait()

    def head_sum(s):
        s = jnp.maximum(s, 0.0) * iw
        return jnp.sum(s.reshape(TS, IDX_HEADS, s.shape[1]), axis=1)

    for g0 in range(0, n_pages, group):
        kt = jnp.concatenate([buf[slot, k].astype(BF16) for k in range(g0, g0 + group)], axis=1)
        o_ref[:, g0 * PAGE_SIZE:(g0 + group) * PAGE_SIZE] = head_sum(_dot(iq, kt))
    o_ref[:, n_pages * PAGE_SIZE:] = head_sum(_dot_nt(iq, ikn_ref[...].astype(BF16)))


def _sample_scores(pt, iq, iw, ik_new, cache_ikt, n_pages):
    DB, R, _ = iq.shape
    TS = R // IDX_HEADS
    W = (n_pages + 1) * PAGE_SIZE
    grid_spec = pltpu.PrefetchScalarGridSpec(
        num_scalar_prefetch=1,
        grid=(DB,),
        in_specs=[pl.BlockSpec((None, R, IDX_DIM), lambda b, pt: (b, 0, 0)),
                  pl.BlockSpec((None, R, 1), lambda b, pt: (b, 0, 0)),
                  pl.BlockSpec((None, PAGE_SIZE, IDX_DIM), lambda b, pt: (b, 0, 0)),
                  pl.BlockSpec(memory_space=pl.ANY)],
        out_specs=pl.BlockSpec((None, TS, W), lambda b, pt: (b, 0, 0)),
        scratch_shapes=[pltpu.VMEM((2, n_pages, IDX_DIM, PAGE_SIZE), cache_ikt.dtype),
                        pltpu.SemaphoreType.DMA((2,))],
    )
    return pl.pallas_call(
        functools.partial(_sample_scores_kernel, n_pages=n_pages, group=min(n_pages, 8)),
        grid_spec=grid_spec,
        out_shape=jax.ShapeDtypeStruct((DB, TS, W), F32),
        compiler_params=_cparams("arbitrary"),
    )(pt, iq, iw, ik_new, cache_ikt)


def _sample_select_kernel(s_ref, o_ref, *, past, ksel, tp):
    RB, W = s_ref.shape
    t = lax.broadcasted_iota(I32, (RB, W), 0) % tp
    col = lax.broadcasted_iota(I32, (RB, W), 1)
    visible = col <= past + t
    x = jnp.where(visible, s_ref[...], -jnp.inf)

    def count_fn(pred, *per_row):
        hit = jnp.where(pred(x, col, *per_row), 1.0, 0.0)
        a = hit[:, :LANES]
        for c in range(1, W // LANES):
            a = a + hit[:, c * LANES:(c + 1) * LANES]
        return jnp.sum(a, axis=1, keepdims=True)

    all_visible, thr, cut = _select_threshold(count_fn, (RB, 1), ksel, W.bit_length())
    sel = visible & (all_visible | (x > thr) | ((x == thr) & (col < cut)))
    o_ref[...] = jnp.where(sel, 1.0, 0.0).astype(o_ref.dtype)


def _sample_select(scores, past, ksel, tp):
    R, W = scores.shape
    rb = min(R, 64)
    return pl.pallas_call(
        functools.partial(_sample_select_kernel, past=past, ksel=ksel, tp=tp),
        grid=(R // rb,),
        in_specs=[pl.BlockSpec((rb, W), lambda i: (i, 0))],
        out_specs=pl.BlockSpec((rb, W), lambda i: (i, 0)),
        out_shape=jax.ShapeDtypeStruct((R, W), BF16),
        compiler_params=_cparams("parallel"),
    )(scores)


def _sample_attend_kernel(pt_ref, qlat_ref, qr_ref, dq_ref, sel_ref, seln_ref,
                          ckvn_ref, krn_ref, dkn_ref, dvn_ref, ckv_hbm, krt_hbm, dk_hbm, dv_hbm,
                          olat_ref, odsa_ref, ma_ref, la_ref, acca_ref, md_ref, ld_ref, accd_ref,
                          ckv_cat, krt_cat, dk_cat, dv_cat, ckv_buf, krt_buf, dk_buf, dv_buf, sem, *, pages, n_steps):
    b, j = pl.program_id(0), pl.program_id(1)
    step = b * n_steps + j
    n_total = pl.num_programs(0) * n_steps
    slot = step % 2
    RA, RD = qlat_ref.shape[0], dq_ref.shape[0]
    states = [(ma_ref, la_ref, acca_ref), (md_ref, ld_ref, accd_ref)]

    def page_copies(step, slot):
        copies = []
        for k in range(pages):
            page = pt_ref[step * pages + k]
            copies += [pltpu.make_async_copy(hbm.at[page], buf.at[slot, k], sem.at[slot, a])
                       for a, (hbm, buf) in enumerate(((ckv_hbm, ckv_buf), (krt_hbm, krt_buf),
                                                       (dk_hbm, dk_buf), (dv_hbm, dv_buf)))]
        return copies

    @pl.when(step == 0)
    def _():
        for c in page_copies(0, 0):
            c.start()

    @pl.when(step + 1 < n_total)
    def _():
        for c in page_copies(step + 1, 1 - slot):
            c.start()

    for c in page_copies(step, slot):
        c.wait()
    ckv_pages, krt_pages, dk_pages, dv_pages = ([buf.at[slot, k] for k in range(pages)]
                                                for buf in (ckv_buf, krt_buf, dk_buf, dv_buf))

    @pl.when(j == 0)
    def _():
        _softmax_init(ma_ref, la_ref, acca_ref)
        _softmax_init(md_ref, ld_ref, accd_ref)

    qlat, qr, dq = qlat_ref[...], qr_ref[...], dq_ref[...]

    def dsa_mask(flags_ref, s):
        flags = flags_ref[...].astype(F32)
        selected = jnp.concatenate([flags] * (RD // flags.shape[0]), axis=0) > 0.5
        head_of_col = lax.broadcasted_iota(I32, selected.shape, 1) % DSA_KV_HEADS
        head_of_row = lax.broadcasted_iota(I32, selected.shape, 0) // (RD // DSA_KV_HEADS)
        return jnp.where(selected & (head_of_col == head_of_row), s, NEG)

    s_mla, s_dsa = [], []
    for c0 in range(0, pages, SAMPLE_CHUNK_PAGES):
        for k in range(c0, min(c0 + SAMPLE_CHUNK_PAGES, pages)):
            ckv_cat[k * PAGE_SIZE:(k + 1) * PAGE_SIZE, :] = ckv_pages[k][...].astype(BF16)
            krt_cat[:, k * PAGE_SIZE:(k + 1) * PAGE_SIZE] = krt_pages[k][...].astype(BF16)
            dk_cat[k * 2 * PAGE_SIZE:(k + 1) * 2 * PAGE_SIZE, :] = dk_pages[k][...].astype(BF16)
            dv_cat[k * 2 * PAGE_SIZE:(k + 1) * 2 * PAGE_SIZE, :] = dv_pages[k][...].astype(BF16)
        keys = slice(c0 * PAGE_SIZE, min(c0 + SAMPLE_CHUNK_PAGES, pages) * PAGE_SIZE)
        rows = slice(2 * keys.start, 2 * keys.stop)
        s_mla.append(_dot_nt(qlat, ckv_cat[keys, :]) + _dot(qr, krt_cat[:, keys]))
        s_dsa.append(_dot_nt(dq, dk_cat[rows, :]))
    _softmax_updates([jnp.concatenate(s_mla, axis=1), dsa_mask(sel_ref, jnp.concatenate(s_dsa, axis=1))],
                     [ckv_cat[...], dv_cat[...]], states)

    @pl.when(j == n_steps - 1)
    def _():
        t = lax.broadcasted_iota(I32, (RA, PAGE_SIZE), 0) // MLA_HEADS
        cn = lax.broadcasted_iota(I32, (RA, PAGE_SIZE), 1)
        ckvn = ckvn_ref[...].astype(BF16)
        s = _dot_nt(qlat, ckvn) + _dot_nt(qr, krn_ref[...].astype(BF16))
        sd = dsa_mask(seln_ref, _dot_nt(dq, dkn_ref[...].astype(BF16)))
        _softmax_updates([jnp.where(cn <= t, s, NEG), sd], [ckvn, dvn_ref[...].astype(BF16)], states)
        olat_ref[...] = acca_ref[...] / _lane_tile(la_ref[...], MLA_D_CKV)
        odsa_ref[...] = accd_ref[...] / ld_ref[...]


def _sample_attend(pt, qlat, qr, dq, sel, sel_new, new_rows, caches, n_pages):
    DB, RA, _ = qlat.shape
    RD = dq.shape[1]
    TP = sel.shape[1]
    pages = min(n_pages, SAMPLE_PAGES_PER_STEP)
    n_steps = n_pages // pages
    page_specs = [pl.BlockSpec(memory_space=pl.ANY)] * len(caches)
    per_seq = lambda shape: pl.BlockSpec((None,) + shape, lambda b, j, pt: (b,) + (0,) * len(shape))
    in_specs = [per_seq((RA, MLA_D_CKV)), per_seq((RA, MLA_D_ROPE)), per_seq((RD, LANES)),
                pl.BlockSpec((None, TP, pages * 2 * PAGE_SIZE), lambda b, j, pt: (b, 0, j)),
                per_seq((TP, 2 * PAGE_SIZE)),
                per_seq((PAGE_SIZE, MLA_D_CKV)), per_seq((PAGE_SIZE, MLA_D_ROPE)),
                per_seq((2 * PAGE_SIZE, LANES)), per_seq((2 * PAGE_SIZE, LANES))]
    n_keys = pages * PAGE_SIZE
    grid_spec = pltpu.PrefetchScalarGridSpec(
        num_scalar_prefetch=1,
        grid=(DB, n_steps),
        in_specs=in_specs + page_specs,
        out_specs=[per_seq((RA, MLA_D_CKV)), per_seq((RD, LANES))],
        scratch_shapes=[pltpu.VMEM((RA, LANES), F32), pltpu.VMEM((RA, LANES), F32), pltpu.VMEM((RA, MLA_D_CKV), F32)]
                       + [pltpu.VMEM((RD, LANES), F32)] * 3
                       + [pltpu.VMEM((n_keys, MLA_D_CKV), BF16), pltpu.VMEM((MLA_D_ROPE, n_keys), BF16),
                          pltpu.VMEM((2 * n_keys, LANES), BF16), pltpu.VMEM((2 * n_keys, LANES), BF16)]
                       + [pltpu.VMEM((2, pages) + c.shape[1:], c.dtype) for c in caches]
                       + [pltpu.SemaphoreType.DMA((2, len(caches)))],
    )
    return pl.pallas_call(
        functools.partial(_sample_attend_kernel, pages=pages, n_steps=n_steps),
        grid_spec=grid_spec,
        out_shape=[jax.ShapeDtypeStruct((DB, RA, MLA_D_CKV), F32), jax.ShapeDtypeStruct((DB, RD, LANES), F32)],
        compiler_params=_cparams("arbitrary", "arbitrary"),
    )(pt, qlat, qr, dq, sel, sel_new, *new_rows, *caches)


def _uv_kernel(o_ref, w_ref, out_ref):
    out_ref[...] = _dot(o_ref[...], w_ref[...]).astype(out_ref.dtype)


def _mla_up_v(olat_h, wuv_t):
    H, T, C = olat_h.shape
    return pl.pallas_call(
        _uv_kernel,
        grid=(H,),
        in_specs=[pl.BlockSpec((None, T, C), lambda h: (h, 0, 0)), pl.BlockSpec((None, C, MLA_D_V), lambda h: (h, 0, 0))],
        out_specs=pl.BlockSpec((T, MLA_D_V), lambda h: (0, h)),
        out_shape=jax.ShapeDtypeStruct((T, H * MLA_D_V), BF16),
        compiler_params=_cparams("parallel"),
    )(olat_h, wuv_t)


def _rope_tables(pos):
    pos = pos.astype(F32)
    T = pos.shape[0]

    def tabs(rot, period):
        half = rot // 2
        inv_freq = ROPE_THETA ** (-jnp.arange(half, dtype=F32) / half)
        ang = pos[:, None] * inv_freq[None, :]
        cos, sin = jnp.cos(ang), jnp.sin(ang)
        one, zero = jnp.ones((T, period - rot), F32), jnp.zeros((T, period - rot), F32)
        zh = jnp.zeros((T, half), F32)
        return (jnp.concatenate([cos, cos, one], 1), jnp.concatenate([-sin, zh, zero], 1),
                jnp.concatenate([zh, sin, zero], 1))

    rep = lambda ts, period: [jnp.tile(t, (1, LANES // period)) for t in ts]
    mla, idx = tabs(MLA_D_ROPE, MLA_D_ROPE), tabs(IDX_ROT, IDX_DIM)
    z = jnp.zeros((T, LANES // 2), F32)
    kri = [jnp.concatenate([mla[0], idx[0]], 1), jnp.concatenate([mla[1], z], 1), jnp.concatenate([mla[2], z], 1),
           jnp.concatenate([z, idx[1]], 1), jnp.concatenate([z, idx[2]], 1)]
    return jnp.concatenate(rep(mla, MLA_D_ROPE) + rep(tabs(DSA_ROT, LANES), LANES) + rep(idx, IDX_DIM) + kri, axis=1)


def _permute_w_in(w_in):
    offs, off = {}, 0
    for name, w in (("mla_cq", MLA_D_CQ), ("mla_ckv", MLA_D_CKV), ("mla_kr", MLA_D_ROPE),
                    ("dsa_q", DSA_HEADS * DSA_HEAD_DIM), ("dsa_k", DSA_KV_HEADS * DSA_HEAD_DIM),
                    ("dsa_v", DSA_KV_HEADS * DSA_HEAD_DIM), ("idx_q", IDX_HEADS * IDX_DIM), ("idx_w", IDX_HEADS),
                    ("idx_k", IDX_DIM), ("mem_q", MEM_HEADS * MEM_HEAD_DIM), ("gates", N_BRANCH * D_MODEL)):
        offs[name] = (off, w)
        off += w
    w_t = jnp.swapaxes(w_in, 0, 1)
    col = lambda n: w_t[offs[n][0]: offs[n][0] + offs[n][1]]
    pad = jnp.zeros((LANES - IDX_HEADS, w_in.shape[0]), w_in.dtype)
    return jnp.concatenate([col("mla_cq"), col("mla_ckv"), col("dsa_q"), col("dsa_k"), col("dsa_v"), col("idx_q"),
                            col("mem_q"), col("mla_kr"), col("idx_k"), col("idx_w"), pad, col("gates")], axis=0)


def kernel(x_prompt, x_sample, cache_mla_ckv, cache_mla_krope, cache_dsa_k, cache_dsa_v, cache_idx_k, cache_mem_k,
           cache_mem_v, page_table, mem_prompt, g_ffn1, w_ffn1_gate, w_ffn1_up, w_ffn1_down, g_mix, w_in, g_mla_cq,
           w_mla_uq, g_mla_q, w_mla_uk, w_mla_uv, g_mla_ckv, g_mla_kr, g_dsa_q, g_dsa_k, g_mem_in, w_mem_kv,
           g_mem_q, g_mem_k, w_o_mla, w_o_dsa, w_o_mem, w_out, g_ffn2, w_ffn2_gate, w_ffn2_up, w_ffn2_down):
    B, S, D = x_prompt.shape
    DB, TS, _ = x_sample.shape
    depth = g_mix.shape[0]
    n_pages = page_table.shape[1]
    past = n_pages * PAGE_SIZE
    TP = SUBLANES
    xp = x_prompt.reshape(B * S, D)
    xs = x_sample.reshape(DB * TS, D)
    pt = page_table.reshape(-1)
    rt_p = _rope_tables(jnp.arange(S))
    rt_s = jnp.tile(_rope_tables(past + jnp.arange(TS)), (DB, 1))
    row = lambda g: g.reshape(1, -1).astype(F32)
    b16 = lambda w: w.astype(BF16)
    st = {k: [] for k in ("ckv_p", "kr_p", "dk_p", "dv_p", "ik_p", "mk_p", "mv_p", "ckv_s", "kr_s", "dk_s", "dv_s", "ik_s")}

    for l in range(depth):
        ffn1 = (row(g_ffn1[l]), b16(w_ffn1_gate[l]), b16(w_ffn1_up[l]), b16(w_ffn1_down[l]))
        ffn2 = (row(g_ffn2[l]), b16(w_ffn2_gate[l]), b16(w_ffn2_up[l]), b16(w_ffn2_down[l]))
        w_in_p = b16(_permute_w_in(w_in[l]))
        gq = g_mla_q[l]
        gains = (row(g_mla_cq[l]), row(gq[:MLA_D_NOPE]), row(jnp.tile(gq[MLA_D_NOPE:], 2)), row(g_mla_ckv[l]),
                 row(jnp.concatenate([g_mla_kr[l], jnp.ones((LANES - MLA_D_ROPE,), F32)])),
                 row(g_dsa_q[l]), row(g_dsa_k[l]), row(g_mem_q[l]))
        wuq = w_mla_uq[l].reshape(MLA_D_CQ, MLA_HEADS, MLA_D_QK)
        wuq = b16(jnp.concatenate([wuq[:, :, :MLA_D_NOPE].reshape(MLA_D_CQ, -1),
                                   wuq[:, :, MLA_D_NOPE:].reshape(MLA_D_CQ, -1)], axis=1))
        wuk, wuv = w_mla_uk[l], w_mla_uv[l]
        w_kv = b16(jnp.concatenate([wuk.transpose(2, 0, 1).reshape(MLA_D_CKV, -1),
                                    wuv.transpose(2, 0, 1).reshape(MLA_D_CKV, -1)], axis=1))
        merge_w = (b16(w_o_mla[l]), b16(w_o_dsa[l]), b16(w_o_mem[l]), b16(w_out[l]))

        mk_p, mv_p = _memory_kv(mem_prompt.reshape(-1, D), row(g_mem_in[l]), b16(w_mem_kv[l]), row(g_mem_k[l]))
        M = mem_prompt.shape[1]
        xp = _ffn_half(xp, *ffn1)
        c = _project(xp, row(g_mix[l]), w_in_p)
        (ckv, kr, dk, dv, ik, iw, qcat, kcat, vb, dq, dkb, dvb, iq, ikb, mq) = _epilogue(
            c, rt_p, gains, wuq, w_kv, sample=False)
        o_mla = _mla_prompt(qcat, kcat, vb, B, S)
        iq3 = iq.reshape(B * S, IDX_HEADS, IDX_DIM).transpose(1, 0, 2)
        o_dsa = _dsa_prompt(iq3, iw, ikb, dq, dkb, dvb, B, S)
        o_mem = _mem_attend(mq.reshape(B, S, -1), b16(mk_p).reshape(B, M, -1),
                            b16(mv_p).reshape(B, M, -1)).reshape(B * S, -1)
        xp = _merge(xp, o_mla, o_dsa, o_mem, c, *merge_w)
        xp = _ffn_half(xp, *ffn2)
        st["ckv_p"].append(ckv.reshape(B, S, MLA_D_CKV))
        st["kr_p"].append(kr.reshape(B, S, MLA_D_ROPE))
        st["dk_p"].append(dk.reshape(B, S, DSA_KV_HEADS, DSA_HEAD_DIM))
        st["dv_p"].append(dv.reshape(B, S, DSA_KV_HEADS, DSA_HEAD_DIM))
        st["ik_p"].append(ik.reshape(B, S, IDX_DIM))
        st["mk_p"].append(mk_p.reshape(B, M, MEM_HEADS, MEM_HEAD_DIM))
        st["mv_p"].append(mv_p.reshape(B, M, MEM_HEADS, MEM_HEAD_DIM))

        xs = _ffn_half(xs, *ffn1)
        c = _project(xs, row(g_mix[l]), w_in_p)
        (ckv, kr, dk, dv, ik, iw, qlat, qr, dq, iq, mq) = _epilogue(c, rt_s, gains, wuq, b16(wuk), sample=True)
        pad_t = lambda a: jnp.pad(a.reshape(DB, TS, -1), ((0, 0), (0, PAGE_SIZE - TS), (0, 0)))
        ksel = min(TOPK_MAX, (past + TS) // 4)
        scores = _sample_scores(pt, iq.reshape(DB, TS * IDX_HEADS, IDX_DIM),
                                iw[:, :IDX_HEADS].reshape(DB, TS * IDX_HEADS, 1),
                                pad_t(ik), jnp.swapaxes(cache_idx_k[l], 1, 2), n_pages)
        sel = _sample_select(scores.reshape(DB * TS, -1), past, ksel, TS).reshape(DB, TS, -1)
        sel = jnp.pad(sel, ((0, 0), (0, TP - TS), (0, 0)))
        dq_s = jnp.pad(dq.reshape(DB, TS, DSA_KV_HEADS, DSA_GROUP, LANES).transpose(0, 2, 3, 1, 4),
                       ((0, 0), (0, 0), (0, 0), (0, TP - TS), (0, 0))).reshape(DB, DSA_HEADS * TP, LANES)
        sel2 = jnp.repeat(sel, DSA_KV_HEADS, axis=2)
        as_page = lambda a: pad_t(a).reshape(DB, DSA_KV_HEADS * PAGE_SIZE, LANES)
        olat, odsa = _sample_attend(
            pt, qlat.reshape(DB, TS * MLA_HEADS, MLA_D_CKV), qr.reshape(DB, TS * MLA_HEADS, MLA_D_ROPE), dq_s,
            sel2[:, :, :DSA_KV_HEADS * past], sel2[:, :, DSA_KV_HEADS * past:],
            (pad_t(ckv), pad_t(kr), as_page(dk), as_page(dv)),
            (cache_mla_ckv[l], jnp.swapaxes(cache_mla_krope[l], 1, 2),
             cache_dsa_k[l].reshape(-1, DSA_KV_HEADS * PAGE_SIZE, LANES),
             cache_dsa_v[l].reshape(-1, DSA_KV_HEADS * PAGE_SIZE, LANES)), n_pages)
        olat_h = b16(olat.reshape(DB * TS, MLA_HEADS, MLA_D_CKV).transpose(1, 0, 2))
        o_mla = _mla_up_v(olat_h, b16(wuv.transpose(0, 2, 1)))
        o_dsa = b16(odsa.reshape(DB, DSA_KV_HEADS, DSA_GROUP, TP, LANES)[:, :, :, :TS]
                    .transpose(0, 3, 1, 2, 4).reshape(DB * TS, DSA_HEADS * LANES))
        halves = MEM_HEAD_DIM // LANES
        mq_s = jnp.pad(mq.reshape(DB, TS, MEM_HEADS, halves, LANES).transpose(0, 3, 2, 1, 4),
                       ((0, 0), (0, 0), (0, 0), (0, TP - TS), (0, 0))).reshape(DB, halves * MEM_HEADS * TP, LANES)
        mem_rows = lambda c: (c[l].reshape(DB, c.shape[2], MEM_HEADS, halves, LANES).transpose(0, 1, 3, 2, 4)
                              .reshape(DB, -1, LANES))
        o_mem = _mem_attend_tiled(mq_s, mem_rows(cache_mem_k), mem_rows(cache_mem_v), TP)
        o_mem = (o_mem.reshape(DB, MEM_HEADS, TP, MEM_HEAD_DIM)[:, :, :TS].transpose(0, 2, 1, 3)
                 .reshape(DB * TS, MEM_HEADS * MEM_HEAD_DIM))
        xs = _merge(xs, o_mla, o_dsa, o_mem, c, *merge_w)
        xs = _ffn_half(xs, *ffn2)
        st["ckv_s"].append(ckv.reshape(DB, TS, MLA_D_CKV))
        st["kr_s"].append(kr.reshape(DB, TS, MLA_D_ROPE))
        st["dk_s"].append(dk.reshape(DB, TS, DSA_KV_HEADS, DSA_HEAD_DIM))
        st["dv_s"].append(dv.reshape(DB, TS, DSA_KV_HEADS, DSA_HEAD_DIM))
        st["ik_s"].append(ik.reshape(DB, TS, IDX_DIM))

    return (xp.reshape(B, S, D), xs.reshape(DB, TS, D),
            jnp.stack(st["ckv_p"]), jnp.stack(st["kr_p"]), jnp.stack(st["dk_p"]), jnp.stack(st["dv_p"]),
            jnp.stack(st["ik_p"]), jnp.stack(st["mk_p"]), jnp.stack(st["mv_p"]),
            jnp.stack(st["ckv_s"]), jnp.stack(st["kr_s"]), jnp.stack(st["dk_s"]), jnp.stack(st["dv_s"]),
            jnp.stack(st["ik_s"]))
```
